```python
import math
import jax, jax.numpy as jnp
from jax import lax
import numpy as np

D_MODEL = 4096
BATCH = 4
SEQ = 2048
DEPTH = 1
DEC_BATCH = 128
DEC_SEQ = 8
PAST_LEN = 16384
PAGE_SIZE = 128

GLA_HEADS = 4
GLA_DV = D_MODEL // 2
GLA_DK = GLA_DV // 2
GLA_HEAD_DK = GLA_DK // GLA_HEADS
GLA_HEAD_DV = GLA_DV // GLA_HEADS
GLA_GATE_RANK = 16
GLA_GATE_TAU = 16.0
GLA_CHUNK = 64
POOL_WIDTH = D_MODEL // 4
POOL_WINDOWS = (2, 4, 8, 16)
POOL_GROUPS = len(POOL_WINDOWS)
POOL_GROUP_DIM = POOL_WIDTH // POOL_GROUPS
POOL_BUF = max(POOL_WINDOWS) - 1
MEM_TOKENS = 256
MEM_HEADS = 4
MEM_WIDTH = D_MODEL // 4
MEM_HEAD_DIM = MEM_WIDTH // MEM_HEADS
N_BRANCHES = 3
N_EXPERTS = 32
TOP_K = 4
D_FF = D_MODEL
SWIGLU_LIMIT = 7.0
SWIGLU_ALPHA = 1.702
MOE_BLOCK = 128
NORM_EPS = 1e-5
DEEPNORM_ALPHA = (2.0 * DEPTH) ** 0.25
DEEPNORM_BETA = (8.0 * DEPTH) ** -0.25
IN_SIZES = (GLA_DK, GLA_DK, GLA_DV, GLA_DV, GLA_GATE_RANK, POOL_WIDTH, MEM_WIDTH, N_BRANCHES * D_MODEL)
IN_WIDTH = sum(IN_SIZES)

kernel_name = 'hybrid_gla_pool_memxattn_moe_step'


def _split_points():
    return [int(s) for s in np.cumsum(IN_SIZES)[:-1]]


def layer_norm(x, g, b):
    xf = x.astype(jnp.float32)
    mu = jnp.mean(xf, axis=-1, keepdims=True)
    var = jnp.mean(jnp.square(xf - mu), axis=-1, keepdims=True)
    return ((xf - mu) * lax.rsqrt(var + NORM_EPS) * g.astype(jnp.float32) + b.astype(jnp.float32)).astype(x.dtype)


def gla_chunk(state, chunk):
    q, k, v, log_a = chunk
    c = q.shape[2]
    cum = jnp.cumsum(log_a, axis=2)
    causal = jnp.tril(jnp.ones((c, c), dtype=bool))
    rel = cum[:, :, :, None, :] - cum[:, :, None, :, :]
    decay = jnp.exp(jnp.where(causal[:, :, None], rel, -jnp.inf))
    scores = jnp.einsum('bhid,bhjd,bhijd->bhij', q, k, decay)
    out = (jnp.einsum('bhij,bhjv->bhiv', scores, v)
           + jnp.einsum('bhid,bhdv->bhiv', q * jnp.exp(cum), state))
    last = cum[:, :, -1, :]
    k_dec = k * jnp.exp(last[:, :, None, :] - cum)
    new_state = jnp.exp(last)[..., None] * state + jnp.einsum('bhjd,bhjv->bhdv', k_dec, v)
    return new_state, out


def gla_scan(q, k, v, log_a, state0):
    bsz, t_len = q.shape[:2]
    c = math.gcd(t_len, GLA_CHUNK)
    n_chunks = t_len // c

    def chunks(a):
        a = a.astype(jnp.float32).reshape(bsz, n_chunks, c, GLA_HEADS, a.shape[-1])
        return jnp.transpose(a, (1, 0, 3, 2, 4))

    state, out = lax.scan(gla_chunk, state0.astype(jnp.float32),
                          (chunks(q), chunks(k), chunks(v), chunks(log_a)))
    out = jnp.transpose(out, (1, 0, 3, 2, 4)).reshape(bsz, t_len, GLA_HEADS, GLA_HEAD_DV)
    return out, state


def pool_mix(u_ext, n_prev, pos0, pool_w, pool_scale):
    bsz, rows, _ = u_ext.shape
    t_new = rows - n_prev
    uf = u_ext.astype(jnp.float32)
    csum = jnp.concatenate([jnp.zeros((bsz, 1, POOL_WIDTH), jnp.float32), jnp.cumsum(uf, axis=1)], axis=1)
    idx = n_prev + jnp.arange(t_new)
    pos = pos0 + jnp.arange(t_new)
    means = []
    for gi, win in enumerate(POOL_WINDOWS):
        ch = slice(gi * POOL_GROUP_DIM, (gi + 1) * POOL_GROUP_DIM)
        lo = jnp.maximum(idx + 1 - win, 0)
        cnt = jnp.minimum(pos + 1, win).astype(jnp.float32)
        means.append((csum[:, idx + 1, ch] - csum[:, lo, ch]) / cnt[None, :, None])
    mean = jnp.stack(means, axis=2)
    tok = uf[:, n_prev:].reshape(bsz, t_new, POOL_GROUPS, POOL_GROUP_DIM)
    mixed = jnp.einsum('btgc,gcd->btgd', mean - tok, pool_w.astype(jnp.float32))
    return (mixed.reshape(bsz, t_new, POOL_WIDTH) * pool_scale.astype(jnp.float32)).astype(u_ext.dtype)


def mem_attend(q, mem_k, mem_v):
    s = jnp.einsum('bthd,bmhd->bhtm', q.astype(jnp.float32), mem_k.astype(jnp.float32)) * MEM_HEAD_DIM ** -0.5
    p = jax.nn.softmax(s, axis=-1)
    return jnp.einsum('bhtm,bmhd->bthd', p, mem_v.astype(jnp.float32)).astype(q.dtype)


def moe_ffn(x, w_router, b_router, w_up, b_up, w_down, b_down):
    n_tok, d = x.shape
    logits = x.astype(jnp.float32) @ w_router.astype(jnp.float32) + b_router.astype(jnp.float32)
    top_val, top_idx = lax.top_k(logits, TOP_K)
    gate = jax.nn.softmax(top_val, axis=-1)
    n_assign = n_tok * TOP_K
    flat_e = top_idx.reshape(-1)
    flat_tok = jnp.repeat(jnp.arange(n_tok, dtype=jnp.int32), TOP_K)
    order = jnp.argsort(flat_e)
    sorted_e = flat_e[order]
    counts = jnp.bincount(flat_e, length=N_EXPERTS)
    padded = (counts + MOE_BLOCK - 1) // MOE_BLOCK * MOE_BLOCK
    start = jnp.cumsum(counts) - counts
    pend = jnp.cumsum(padded)
    pstart = pend - padded
    dest = pstart[sorted_e] + (jnp.arange(n_assign) - start[sorted_e])
    n_blocks = (n_assign + N_EXPERTS * (MOE_BLOCK - 1) + MOE_BLOCK - 1) // MOE_BLOCK
    cap = n_blocks * MOE_BLOCK
    slot_tok = jnp.full((cap,), n_tok, jnp.int32).at[dest].set(flat_tok[order])
    slot_gate = jnp.zeros((cap,), jnp.float32).at[dest].set(gate.reshape(-1)[order])
    block_expert = jnp.minimum(jnp.searchsorted(pend, jnp.arange(n_blocks) * MOE_BLOCK, side='right'), N_EXPERTS - 1)
    x_pad = jnp.concatenate([x, jnp.zeros((1, d), x.dtype)], axis=0)
    xb = x_pad[slot_tok].reshape(n_blocks, MOE_BLOCK, d)

    def expert_block(args):
        xe, e = args
        h = xe @ w_up[e] + b_up[e]
        h_glu = jnp.minimum(h[:, 0::2], SWIGLU_LIMIT)
        h_lin = jnp.clip(h[:, 1::2], -SWIGLU_LIMIT, SWIGLU_LIMIT)
        act = h_glu * jax.nn.sigmoid(SWIGLU_ALPHA * h_glu) * (h_lin + 1.0)
        return act @ w_down[e] + b_down[e]

    yb = lax.map(expert_block, (xb, block_expert))
    contrib = yb.reshape(cap, d).astype(jnp.float32) * slot_gate[:, None]
    y = jax.ops.segment_sum(contrib, slot_tok, num_segments=n_tok + 1)[:n_tok]
    return y.astype(x.dtype)


def trunk_layer(x, mem_k, mem_v, gla_state, pool_buf, pos0,
                w_in, w_fg, b_fg, gla_norm_g, pool_w, pool_scale, w_br_gla, w_br_pool, w_br_mem,
                b_gate, w_out, ln1_g, ln1_b, w_router, b_router, w_up, b_up, w_down, b_down, ln2_g, ln2_b):
    bsz, t_len, _ = x.shape
    z = x @ w_in
    q, k, v, r, fg_low, u, mq, g_logit = jnp.split(z, _split_points(), axis=-1)
    q = q.reshape(bsz, t_len, GLA_HEADS, GLA_HEAD_DK) * GLA_HEAD_DK ** -0.5
    k = k.reshape(bsz, t_len, GLA_HEADS, GLA_HEAD_DK)
    v = v.reshape(bsz, t_len, GLA_HEADS, GLA_HEAD_DV)
    log_a = jax.nn.log_sigmoid((fg_low @ w_fg + b_fg).astype(jnp.float32)) / GLA_GATE_TAU
    log_a = log_a.reshape(bsz, t_len, GLA_HEADS, GLA_HEAD_DK)
    o, new_gla = gla_scan(q, k, v, log_a, gla_state)
    o = o * lax.rsqrt(jnp.mean(jnp.square(o), axis=-1, keepdims=True) + NORM_EPS) * gla_norm_g.astype(jnp.float32)
    o = (o.reshape(bsz, t_len, GLA_DV) * jax.nn.silu(r.astype(jnp.float32))).astype(x.dtype)
    br_a = o @ w_br_gla
    u_ext = jnp.concatenate([pool_buf.astype(u.dtype), u], axis=1)
    br_b = pool_mix(u_ext, pool_buf.shape[1], pos0, pool_w, pool_scale) @ w_br_pool
    br_c = mem_attend(mq.reshape(bsz, t_len, MEM_HEADS, MEM_HEAD_DIM), mem_k, mem_v).reshape(bsz, t_len, MEM_WIDTH) @ w_br_mem
    gates = jax.nn.sigmoid((g_logit + b_gate).astype(jnp.float32)).reshape(bsz, t_len, N_BRANCHES, D_MODEL)
    h = (gates[:, :, 0] * br_a + gates[:, :, 1] * br_b + gates[:, :, 2] * br_c).astype(x.dtype)
    x1 = layer_norm(DEEPNORM_ALPHA * x + h @ w_out, ln1_g, ln1_b)
    ffn = moe_ffn(x1.reshape(bsz * t_len, D_MODEL), w_router, b_router, w_up, b_up, w_down, b_down)
    x2 = layer_norm(DEEPNORM_ALPHA * x1 + ffn.reshape(bsz, t_len, D_MODEL), ln2_g, ln2_b)
    return x2, new_gla.astype(gla_state.dtype), u_ext[:, -POOL_BUF:]


def setup_inputs(seed: int = 0) -> dict:
    key = jax.random.key(seed)
    ks = jax.random.split(key, 30)
    L = DEPTH

    def nrm(k, shape, scale):
        return jax.random.normal(k, shape, jnp.float32) * scale

    return {
        'x_prompt': nrm(ks[0], (BATCH, SEQ, D_MODEL), 1.0),
        'x_sample': nrm(ks[1], (DEC_BATCH, DEC_SEQ, D_MODEL), 1.0),
        'mem_prompt': nrm(ks[2], (BATCH, MEM_TOKENS, D_MODEL), 1.0),
        'cache_mem_k': nrm(ks[3], (L, DEC_BATCH, MEM_TOKENS, MEM_HEADS, MEM_HEAD_DIM), 1.0),
        'cache_mem_v': nrm(ks[4], (L, DEC_BATCH, MEM_TOKENS, MEM_HEADS, MEM_HEAD_DIM), 1.0),
        'state_gla': nrm(ks[5], (L, DEC_BATCH, GLA_HEADS, GLA_HEAD_DK, GLA_HEAD_DV), 2.0),
        'state_pool': nrm(ks[6], (L, DEC_BATCH, POOL_BUF, POOL_WIDTH), 1.0),
        'w_in': nrm(ks[7], (L, D_MODEL, IN_WIDTH), D_MODEL ** -0.5),
        'w_fg': nrm(ks[8], (L, GLA_GATE_RANK, GLA_DK), GLA_GATE_RANK ** -0.5),
        'b_fg': nrm(ks[9], (L, GLA_DK), 0.1),
        'gla_norm_g': 1.0 + nrm(ks[10], (L, GLA_HEADS, GLA_HEAD_DV), 0.02),
        'pool_w': nrm(ks[11], (L, POOL_GROUPS, POOL_GROUP_DIM, POOL_GROUP_DIM), POOL_GROUP_DIM ** -0.5),
        'pool_scale': 1.0 + nrm(ks[12], (L, POOL_WIDTH), 0.1),
        'w_mem_k': nrm(ks[13], (L, D_MODEL, MEM_WIDTH), D_MODEL ** -0.5),
        'w_mem_v': nrm(ks[14], (L, D_MODEL, MEM_WIDTH), D_MODEL ** -0.5),
        'w_br_gla': nrm(ks[15], (L, GLA_DV, D_MODEL), GLA_DV ** -0.5),
        'w_br_pool': nrm(ks[16], (L, POOL_WIDTH, D_MODEL), POOL_WIDTH ** -0.5),
        'w_br_mem': nrm(ks[17], (L, MEM_WIDTH, D_MODEL), MEM_WIDTH ** -0.5),
        'b_gate': nrm(ks[18], (L, N_BRANCHES * D_MODEL), 0.01),
        'w_out': nrm(ks[19], (L, D_MODEL, D_MODEL), D_MODEL ** -0.5 * DEEPNORM_BETA),
        'ln1_g': 1.0 + nrm(ks[20], (L, D_MODEL), 0.02),
        'ln1_b': nrm(ks[21], (L, D_MODEL), 0.02),
        'w_router': nrm(ks[22], (L, D_MODEL, N_EXPERTS), D_MODEL ** -0.5),
        'b_router': nrm(ks[23], (L, N_EXPERTS), 0.01),
        'w_up': nrm(ks[24], (L, N_EXPERTS, D_MODEL, 2 * D_FF), D_MODEL ** -0.5),
        'b_up': nrm(ks[25], (L, N_EXPERTS, 2 * D_FF), 0.02),
        'w_down': nrm(ks[26], (L, N_EXPERTS, D_FF, D_MODEL), D_FF ** -0.5 * DEEPNORM_BETA),
        'b_down': nrm(ks[27], (L, N_EXPERTS, D_MODEL), 0.02),
        'ln2_g': 1.0 + nrm(ks[28], (L, D_MODEL), 0.02),
        'ln2_b': nrm(ks[29], (L, D_MODEL), 0.02),
    }


def reference(x_prompt, x_sample, mem_prompt, cache_mem_k, cache_mem_v, state_gla, state_pool,
              w_in, w_fg, b_fg, gla_norm_g, pool_w, pool_scale, w_mem_k, w_mem_v, w_br_gla, w_br_pool, w_br_mem,
              b_gate, w_out, ln1_g, ln1_b, w_router, b_router, w_up, b_up, w_down, b_down, ln2_g, ln2_b):
    y_p, y_s = x_prompt, x_sample
    bp = x_prompt.shape[0]
    mk_p, mv_p, gla_p, pool_p, gla_s, pool_s = [], [], [], [], [], []
    for l in range(DEPTH):
        lw = (w_in[l], w_fg[l], b_fg[l], gla_norm_g[l], pool_w[l], pool_scale[l], w_br_gla[l], w_br_pool[l],
              w_br_mem[l], b_gate[l], w_out[l], ln1_g[l], ln1_b[l], w_router[l], b_router[l], w_up[l], b_up[l],
              w_down[l], b_down[l], ln2_g[l], ln2_b[l])
        mk = (mem_prompt @ w_mem_k[l]).reshape(bp, MEM_TOKENS, MEM_HEADS, MEM_HEAD_DIM)
        mv = (mem_prompt @ w_mem_v[l]).reshape(bp, MEM_TOKENS, MEM_HEADS, MEM_HEAD_DIM)
        s0 = jnp.zeros((bp, GLA_HEADS, GLA_HEAD_DK, GLA_HEAD_DV), x_prompt.dtype)
        buf0 = jnp.zeros((bp, 0, POOL_WIDTH), x_prompt.dtype)
        y_p, sp, bufp = trunk_layer(y_p, mk, mv, s0, buf0, 0, *lw)
        y_s, ss, bufs = trunk_layer(y_s, cache_mem_k[l], cache_mem_v[l], state_gla[l], state_pool[l], PAST_LEN, *lw)
        mk_p.append(mk)
        mv_p.append(mv)
        gla_p.append(sp)
        pool_p.append(bufp)
        gla_s.append(ss)
        pool_s.append(bufs)
    return (y_p, y_s, jnp.stack(mk_p), jnp.stack(mv_p), jnp.stack(gla_p), jnp.stack(pool_p), jnp.stack(gla_s), jnp.stack(pool_s))
```

```python
import functools
import math

import jax
import jax.numpy as jnp
from jax import lax
from jax.experimental import pallas as pl
from jax.experimental.pallas import tpu as pltpu

D_MODEL = 4096
DEPTH = 1
GLA_HEADS = 4
GLA_DV = D_MODEL // 2
GLA_DK = GLA_DV // 2
GLA_HEAD_DK = GLA_DK // GLA_HEADS
GLA_HEAD_DV = GLA_DV // GLA_HEADS
GLA_GATE_RANK = 16
GLA_GATE_TAU = 16.0
POOL_WIDTH = D_MODEL // 4
POOL_WINDOWS = (2, 4, 8, 16)
POOL_GROUP_DIM = POOL_WIDTH // len(POOL_WINDOWS)
POOL_BUF = max(POOL_WINDOWS) - 1
MEM_HEADS = 4
MEM_WIDTH = D_MODEL // 4
MEM_HEAD_DIM = MEM_WIDTH // MEM_HEADS
N_BRANCHES = 3
N_EXPERTS = 32
TOP_K = 4
D_FF = D_MODEL
SWIGLU_LIMIT = 7.0
SWIGLU_ALPHA = 1.702
NORM_EPS = 1e-5
DEEPNORM_ALPHA = (2.0 * DEPTH) ** 0.25
PAST_LEN = 16384

COL_Q = 0
COL_K = GLA_DK
COL_V = 2 * GLA_DK
COL_R = COL_V + GLA_DV
COL_U = COL_R + GLA_DV
COL_MQ = COL_U + POOL_WIDTH
COL_G = COL_MQ + MEM_WIDTH
Z_WIDTH = COL_G + N_BRANCHES * D_MODEL
FG_COL0 = COL_R + GLA_DV
FG_PAD = 128

V7X_VMEM_BYTES = 64 * 1024 * 1024
V7X_VMEM_REQUEST_CAP = 56 * 1024 * 1024
LANE = 128

GLA_SUB = 16
MOE_BM = 256
MOE_TN = 512
HALO = 16

_HI = lax.Precision.HIGHEST
_F32 = jnp.float32
_BF16 = jnp.bfloat16


def _nbytes(shape, dtype):
    return math.prod(shape) * jnp.dtype(dtype).itemsize


def _params(semantics, block_bytes, extra_bytes=0):
    need = 2 * block_bytes + extra_bytes + (4 << 20)
    return pltpu.CompilerParams(dimension_semantics=semantics,
                                vmem_limit_bytes=int(min(max(need, 16 << 20), V7X_VMEM_REQUEST_CAP)))


def _mm_body(x_ref, w_ref, o_ref):
    o_ref[...] = jnp.dot(x_ref[...], w_ref[...], preferred_element_type=_F32).astype(o_ref.dtype)


def _matmul(x, w, bm, bn, out_dtype=_F32, name="matmul"):
    m, k = x.shape
    _, n = w.shape
    assert m % bm == 0 and n % bn == 0
    blk = _nbytes((bm, k), x.dtype) + _nbytes((k, bn), w.dtype) + _nbytes((bm, bn), out_dtype)
    return pl.pallas_call(
        _mm_body,
        grid=(m // bm, n // bn),
        in_specs=[pl.BlockSpec((bm, k), lambda i, j: (i, 0)),
                  pl.BlockSpec((k, bn), lambda i, j: (0, j))],
        out_specs=pl.BlockSpec((bm, bn), lambda i, j: (i, j)),
        out_shape=jax.ShapeDtypeStruct((m, n), out_dtype),
        compiler_params=_params(("parallel", "arbitrary"), blk, _nbytes((bm, bn), _F32)),
        name=name,
    )(x, w)


def _log_sigmoid(x):
    return jnp.minimum(x, 0.0) - jnp.log(1.0 + jnp.exp(-jnp.abs(x)))


def _gla_body(q_ref, k_ref, v_ref, r_ref, fg_ref, wfg_ref, bfg_ref, g_ref, s0_ref,
              o_ref, s_out_ref,
              st_scr, q_scr, cum_scr, qd_scr, kd_scr, tot_scr, o_scr, *, tb, c):
    t_blk = pl.program_id(2)
    n_sub = tb // c
    shift = int(math.log2(c))

    @pl.when(t_blk == 0)
    def _():
        st_scr[...] = s0_ref[...].T

    fg = jnp.dot(fg_ref[...], wfg_ref[...], precision=_HI, preferred_element_type=_F32) + bfg_ref[...]
    la = _log_sigmoid(fg) / GLA_GATE_TAU
    row = lax.broadcasted_iota(jnp.int32, (tb, tb), 0)
    col = lax.broadcasted_iota(jnp.int32, (tb, tb), 1)
    same = jnp.right_shift(row, shift) == jnp.right_shift(col, shift)
    incl = jnp.where(same & (col <= row), 1.0, 0.0).astype(_F32)
    cum = jnp.dot(incl, la, precision=_HI, preferred_element_type=_F32)
    tot = jnp.dot(same.astype(_F32), la, precision=_HI, preferred_element_type=_F32)
    q = q_ref[...] * (GLA_HEAD_DK ** -0.5)
    q_scr[...] = q
    cum_scr[...] = cum
    tot_scr[...] = tot
    qd_scr[...] = q * jnp.exp(cum)
    kd_scr[...] = k_ref[...] * jnp.exp(tot - cum)

    def sub_chunk(s, carry):
        r0 = pl.multiple_of(s * c, c)
        rows = pl.ds(r0, c)
        q_s = q_scr[rows, :]
        k_s = k_ref[rows, :]
        v_s = v_ref[rows, :]
        cum_s = cum_scr[rows, :]
        j_idx = lax.broadcasted_iota(jnp.int32, (c, 1), 0)
        outs = []
        for i in range(c):
            rel = cum_s[i:i + 1, :] - cum_s
            decay = jnp.exp(jnp.where(j_idx <= i, rel, -jnp.inf))
            score = jnp.sum(decay * k_s * q_s[i:i + 1, :], axis=1, keepdims=True)
            outs.append(jnp.sum(score * v_s, axis=0, keepdims=True))
        o_diag = jnp.concatenate(outs, axis=0)
        st = st_scr[...]
        o_state = lax.dot_general(qd_scr[rows, :].astype(_BF16), st.astype(_BF16),
                                  (((1,), (1,)), ((), ())), preferred_element_type=_F32)
        o_scr[rows, :] = o_diag + o_state
        upd = lax.dot_general(v_s.astype(_BF16), kd_scr[rows, :].astype(_BF16),
                              (((0,), (0,)), ((), ())), preferred_element_type=_F32)
        st_scr[...] = st * jnp.exp(tot_scr[pl.ds(r0, 1), :]) + upd
        return carry

    lax.fori_loop(0, n_sub, sub_chunk, 0)

    o = o_scr[...]
    o = o * lax.rsqrt(jnp.mean(o * o, axis=-1, keepdims=True) + NORM_EPS) * g_ref[...]
    r = r_ref[...]
    o_ref[...] = o * (r * jax.nn.sigmoid(r))

    @pl.when(t_blk == pl.num_programs(2) - 1)
    def _():
        s_out_ref[...] = st_scr[...].T


def _gla(z, fgl, wfg_pad, b_fg, gla_g, s0, *, row0, bsz, t_len, tb, c):
    nt = t_len // tb
    rb0 = row0 // tb
    dk, dv = GLA_HEAD_DK, GLA_HEAD_DV

    def rows(b, h, t):
        return rb0 + b * nt + t

    kernel = functools.partial(_gla_body, tb=tb, c=c)
    blk = (2 * _nbytes((tb, dk), _F32) + 2 * _nbytes((tb, dv), _F32) + _nbytes((tb, FG_PAD), _F32)
           + _nbytes((FG_PAD, dk), _F32) + 2 * _nbytes((dk, dv), _F32) + _nbytes((tb, dv), _F32))
    scr = _nbytes((dv, dk), _F32) + 5 * _nbytes((tb, dk), _F32) + _nbytes((tb, dv), _F32)
    return pl.pallas_call(
        kernel,
        grid=(bsz, GLA_HEADS, nt),
        in_specs=[
            pl.BlockSpec((tb, dk), lambda b, h, t: (rows(b, h, t), COL_Q // dk + h)),
            pl.BlockSpec((tb, dk), lambda b, h, t: (rows(b, h, t), COL_K // dk + h)),
            pl.BlockSpec((tb, dv), lambda b, h, t: (rows(b, h, t), COL_V // dv + h)),
            pl.BlockSpec((tb, dv), lambda b, h, t: (rows(b, h, t), COL_R // dv + h)),
            pl.BlockSpec((tb, FG_PAD), lambda b, h, t: (rows(b, h, t), 0)),
            pl.BlockSpec((FG_PAD, dk), lambda b, h, t: (0, h)),
            pl.BlockSpec((1, dk), lambda b, h, t: (0, h)),
            pl.BlockSpec((1, dv), lambda b, h, t: (0, h)),
            pl.BlockSpec((None, None, dk, dv), lambda b, h, t: (b, h, 0, 0)),
        ],
        out_specs=[
            pl.BlockSpec((tb, dv), lambda b, h, t: (b * nt + t, h)),
            pl.BlockSpec((None, None, dk, dv), lambda b, h, t: (b, h, 0, 0)),
        ],
        out_shape=[jax.ShapeDtypeStruct((bsz * t_len, GLA_DV), _F32),
                   jax.ShapeDtypeStruct((bsz, GLA_HEADS, dk, dv), _F32)],
        scratch_shapes=[pltpu.VMEM((dv, dk), _F32), pltpu.VMEM((tb, dk), _F32), pltpu.VMEM((tb, dk), _F32),
                        pltpu.VMEM((tb, dk), _F32), pltpu.VMEM((tb, dk), _F32), pltpu.VMEM((tb, dk), _F32),
                        pltpu.VMEM((tb, dv), _F32)],
        compiler_params=_params(("parallel", "parallel", "arbitrary"), blk, scr + (8 << 20)),
        name=f"gla_t{tb}",
    )(z, z, z, z, fgl, wfg_pad, b_fg, gla_g, s0)


def _pool_body(u_ref, halo_ref, pw_ref, scale_ref, o_ref, ext_scr, *, tb, pos0, first_block_has_no_history):
    t_blk = pl.program_id(1)
    halo = halo_ref[...]
    if first_block_has_no_history:
        halo = jnp.where(t_blk > 0, halo, 0.0)
    ext_scr[0:HALO, :] = halo
    ext_scr[HALO:HALO + tb, :] = u_ref[...]
    pos = pos0 + t_blk * tb + lax.broadcasted_iota(jnp.int32, (tb, 1), 0)
    gd = POOL_GROUP_DIM
    for gi, win in enumerate(POOL_WINDOWS):
        cols = slice(gi * gd, (gi + 1) * gd)
        tok = ext_scr[HALO:HALO + tb, cols]
        acc = tok
        for s in range(1, win):
            acc = acc + ext_scr[HALO - s:HALO - s + tb, cols]
        cnt = jnp.minimum(pos + 1, win).astype(_F32)
        diff = acc / cnt - tok
        mixed = jnp.dot(diff.astype(_BF16), pw_ref[gi].astype(_BF16), preferred_element_type=_F32)
        o_ref[:, cols] = mixed * scale_ref[:, cols]


def _pool(z, halo, pool_w, pool_scale, *, row0, bsz, t_len, tb, pos0, halo_from_z):
    nt = t_len // tb
    rb0 = row0 // tb
    ucol = COL_U // POOL_WIDTH
    kernel = functools.partial(_pool_body, tb=tb, pos0=pos0, first_block_has_no_history=halo_from_z)
    if halo_from_z:
        hb0 = row0 // HALO
        per = tb // HALO
        halo_spec = pl.BlockSpec((HALO, POOL_WIDTH),
                                 lambda b, t: (jnp.maximum(hb0 + (b * nt + t) * per - 1, 0), ucol))
    else:
        halo_spec = pl.BlockSpec((None, HALO, POOL_WIDTH), lambda b, t: (b, 0, 0))
    blk = (2 * _nbytes((tb, POOL_WIDTH), _F32) + _nbytes((HALO, POOL_WIDTH), _F32)
           + _nbytes(pool_w.shape, _F32))
    return pl.pallas_call(
        kernel,
        grid=(bsz, nt),
        in_specs=[
            pl.BlockSpec((tb, POOL_WIDTH), lambda b, t: (rb0 + b * nt + t, ucol)),
            halo_spec,
            pl.BlockSpec(pool_w.shape, lambda b, t: (0, 0, 0)),
            pl.BlockSpec((1, POOL_WIDTH), lambda b, t: (0, 0)),
        ],
        out_specs=pl.BlockSpec((tb, POOL_WIDTH), lambda b, t: (b * nt + t, 0)),
        out_shape=jax.ShapeDtypeStruct((bsz * t_len, POOL_WIDTH), _F32),
        scratch_shapes=[pltpu.VMEM((HALO + tb, POOL_WIDTH), _F32)],
        compiler_params=_params(("parallel", "arbitrary"), blk, 4 * _nbytes((tb, POOL_WIDTH), _F32)),
        name=f"pool_t{tb}",
    )(z, halo, pool_w, pool_scale)


def _memattn_body(q_ref, k_ref, v_ref, o_ref):
    hd = MEM_HEAD_DIM
    for h in range(MEM_HEADS):
        cols = slice(h * hd, (h + 1) * hd)
        q = (q_ref[:, cols] * (hd ** -0.5)).astype(_BF16)
        k = k_ref[:, cols].astype(_BF16)
        s = lax.dot_general(q, k, (((1,), (1,)), ((), ())), preferred_element_type=_F32)
        p = jnp.exp(s - jnp.max(s, axis=-1, keepdims=True))
        p = p / jnp.sum(p, axis=-1, keepdims=True)
        o_ref[:, cols] = jnp.dot(p.astype(_BF16), v_ref[:, cols].astype(_BF16), preferred_element_type=_F32)


def _memattn(z, mem_k, mem_v, *, row0, bsz, t_len, tb):
    nt = t_len // tb
    rb0 = row0 // tb
    m_tok = mem_k.shape[1]
    blk = 2 * _nbytes((tb, MEM_WIDTH), _F32) + 2 * _nbytes((m_tok, MEM_WIDTH), _F32)
    return pl.pallas_call(
        _memattn_body,
        grid=(bsz, nt),
        in_specs=[
            pl.BlockSpec((tb, MEM_WIDTH), lambda b, t: (rb0 + b * nt + t, COL_MQ // MEM_WIDTH)),
            pl.BlockSpec((None, m_tok, MEM_WIDTH), lambda b, t: (b, 0, 0)),
            pl.BlockSpec((None, m_tok, MEM_WIDTH), lambda b, t: (b, 0, 0)),
        ],
        out_specs=pl.BlockSpec((tb, MEM_WIDTH), lambda b, t: (b * nt + t, 0)),
        out_shape=jax.ShapeDtypeStruct((bsz * t_len, MEM_WIDTH), _F32),
        compiler_params=_params(("parallel", "arbitrary"), blk, 6 * _nbytes((tb, m_tok), _F32)),
        name=f"memattn_t{tb}",
    )(z, mem_k, mem_v)


def _merge_body(a_ref, b_ref, c_ref, wa_ref, wb_ref, wc_ref, g0_ref, g1_ref, g2_ref,
                bg0_ref, bg1_ref, bg2_ref, o_ref):
    def gated(x_ref, w_ref, g_ref, bg_ref):
        br = jnp.dot(x_ref[...], w_ref[...], preferred_element_type=_F32)
        return jax.nn.sigmoid(g_ref[...] + bg_ref[...]) * br

    h = gated(a_ref, wa_ref, g0_ref, bg0_ref)
    h = h + gated(b_ref, wb_ref, g1_ref, bg1_ref)
    h = h + gated(c_ref, wc_ref, g2_ref, bg2_ref)
    o_ref[...] = h.astype(o_ref.dtype)


def _merge(br_a, br_b, br_c, wa, wb, wc, z, b_gate, bm, bn):
    n = br_a.shape[0]
    nj = D_MODEL // bn
    g0 = COL_G // bn

    def gspec(i_br):
        return pl.BlockSpec((bm, bn), lambda i, j: (i, g0 + i_br * nj + j))

    def bspec(i_br):
        return pl.BlockSpec((1, bn), lambda i, j: (0, i_br * nj + j))

    def xspec(width):
        return pl.BlockSpec((bm, width), lambda i, j: (i, 0))

    def wspec(width):
        return pl.BlockSpec((width, bn), lambda i, j: (0, j))

    blk = (_nbytes((bm, D_MODEL), _BF16) + _nbytes((D_MODEL, bn), _BF16) + 3 * _nbytes((bm, bn), _F32)
           + _nbytes((bm, bn), _BF16))
    return pl.pallas_call(
        _merge_body,
        grid=(n // bm, nj),
        in_specs=[xspec(GLA_DV), xspec(POOL_WIDTH), xspec(MEM_WIDTH),
                  wspec(GLA_DV), wspec(POOL_WIDTH), wspec(MEM_WIDTH),
                  gspec(0), gspec(1), gspec(2), bspec(0), bspec(1), bspec(2)],
        out_specs=pl.BlockSpec((bm, bn), lambda i, j: (i, j)),
        out_shape=jax.ShapeDtypeStruct((n, D_MODEL), _BF16),
        compiler_params=_params(("parallel", "arbitrary"), blk, 4 * _nbytes((bm, bn), _F32)),
        name="merge",
    )(br_a, br_b, br_c, wa, wb, wc, z, z, z, b_gate, b_gate, b_gate)


def _layer_norm(v, g, b):
    mu = jnp.mean(v, axis=-1, keepdims=True)
    d = v - mu
    var = jnp.mean(d * d, axis=-1, keepdims=True)
    return d * lax.rsqrt(var + NORM_EPS) * g + b


def _ln_router_body(x_ref, t_ref, g_ref, b_ref, wr_ref, br_ref, x1_ref, idx_ref, gate_ref):
    x1 = _layer_norm(DEEPNORM_ALPHA * x_ref[...] + t_ref[...], g_ref[...], b_ref[...])
    x1_ref[...] = x1
    logits = jnp.dot(x1, wr_ref[...], precision=_HI, preferred_element_type=_F32) + br_ref[...]
    bm = logits.shape[0]
    e_idx = lax.broadcasted_iota(jnp.int32, logits.shape, 1)
    lane = lax.broadcasted_iota(jnp.int32, (bm, LANE), 1)
    vals, idxs = [], []
    for _ in range(TOP_K):
        m = jnp.max(logits, axis=-1, keepdims=True)
        first = jnp.min(jnp.where(logits == m, e_idx, N_EXPERTS), axis=-1, keepdims=True)
        vals.append(m)
        idxs.append(first)
        logits = jnp.where(e_idx == first, -jnp.inf, logits)
    exps = [jnp.exp(v - vals[0]) for v in vals]
    denom = exps[0] + exps[1] + exps[2] + exps[3]
    idx_out = jnp.zeros((bm, LANE), jnp.int32)
    gate_out = jnp.zeros((bm, LANE), _F32)
    for k_sel in range(TOP_K):
        idx_out = jnp.where(lane == k_sel, idxs[k_sel], idx_out)
        gate_out = jnp.where(lane == k_sel, exps[k_sel] / denom, gate_out)
    idx_ref[...] = idx_out
    gate_ref[...] = gate_out


def _ln_router(x, t, ln_g, ln_b, w_router, b_router, bm):
    n = x.shape[0]
    row = pl.BlockSpec((bm, D_MODEL), lambda i: (i, 0))
    vec = pl.BlockSpec((1, D_MODEL), lambda i: (0, 0))
    small = pl.BlockSpec((bm, LANE), lambda i: (i, 0))
    blk = 3 * _nbytes((bm, D_MODEL), _F32) + _nbytes((D_MODEL, LANE), _F32)
    return pl.pallas_call(
        _ln_router_body,
        grid=(n // bm,),
        in_specs=[row, row, vec, vec,
                  pl.BlockSpec((D_MODEL, N_EXPERTS), lambda i: (0, 0)),
                  pl.BlockSpec((1, N_EXPERTS), lambda i: (0, 0))],
        out_specs=[row, small, small],
        out_shape=[jax.ShapeDtypeStruct((n, D_MODEL), _F32),
                   jax.ShapeDtypeStruct((n, LANE), jnp.int32), jax.ShapeDtypeStruct((n, LANE), _F32)],
        compiler_params=_params(("parallel",), blk, 8 * _nbytes((bm, D_MODEL), _F32)),
        name="ln1_router",
    )(x, t, ln_g, ln_b, w_router, b_router)


def _row_copy(src_hbm, row, dst, r, sem):
    return pltpu.make_async_copy(src_hbm.at[pl.ds(row, 1), :], dst.at[pl.ds(r, 1), :], sem)


def _dispatch_body(nlive_ref, tok_ref, x_hbm, o_ref, buf, sem, *, gb):
    i = pl.program_id(0)

    @pl.when(i < nlive_ref[0])
    def _():
        def issue(r, carry):
            _row_copy(x_hbm, tok_ref[0, 0, r], buf, r, sem).start()
            return carry

        def drain(r, carry):
            _row_copy(x_hbm, 0, buf, r, sem).wait()
            return carry

        lax.fori_loop(0, gb, issue, 0)
        lax.fori_loop(0, gb, drain, 0)
        o_ref[...] = buf[...].astype(o_ref.dtype)

    @pl.when(i >= nlive_ref[0])
    def _():
        o_ref[...] = jnp.zeros(o_ref.shape, o_ref.dtype)


def _dispatch(x1, slot_tok, n_live_blocks, gb):
    p = slot_tok.shape[0]
    nb = p // gb
    tok3 = slot_tok.reshape(nb, 1, gb)
    blk = _nbytes((gb, D_MODEL), _BF16)
    return pl.pallas_call(
        functools.partial(_dispatch_body, gb=gb),
        grid_spec=pltpu.PrefetchScalarGridSpec(
            num_scalar_prefetch=1,
            grid=(nb,),
            in_specs=[pl.BlockSpec((1, 1, gb), lambda i, nl: (i, 0, 0), memory_space=pltpu.SMEM),
                      pl.BlockSpec(memory_space=pl.ANY)],
            out_specs=pl.BlockSpec((gb, D_MODEL), lambda i, nl: (i, 0)),
            scratch_shapes=[pltpu.VMEM((gb, D_MODEL), _F32), pltpu.SemaphoreType.DMA(())],
        ),
        out_shape=jax.ShapeDtypeStruct((p, D_MODEL), _BF16),
        compiler_params=_params(("arbitrary",), blk, 2 * _nbytes((gb, D_MODEL), _F32)),
        name="moe_dispatch",
    )(n_live_blocks, tok3, x1)


def _up_body(we_ref, wjw_ref, wr_ref, wjo_ref, live_ref, x_ref, wg_ref, wl_ref, bg_ref, bl_ref, o_ref):
    live = live_ref[pl.program_id(0)] == 1

    @pl.when(jnp.logical_not(live))
    def _():
        o_ref[...] = jnp.zeros(o_ref.shape, o_ref.dtype)

    @pl.when(live)
    def _():
        x = x_ref[...]
        h_glu = jnp.dot(x, wg_ref[...], preferred_element_type=_F32) + bg_ref[...]
        h_lin = jnp.dot(x, wl_ref[...], preferred_element_type=_F32) + bl_ref[...]
        h_glu = jnp.minimum(h_glu, SWIGLU_LIMIT)
        h_lin = jnp.clip(h_lin, -SWIGLU_LIMIT, SWIGLU_LIMIT)
        act = h_glu * jax.nn.sigmoid(SWIGLU_ALPHA * h_glu) * (h_lin + 1.0)
        o_ref[...] = act.astype(o_ref.dtype)


def _down_body(we_ref, wjw_ref, wr_ref, wjo_ref, live_ref, x_ref, wd_ref, bd_ref, gate_ref, o_ref):
    live = live_ref[pl.program_id(0)] == 1

    @pl.when(jnp.logical_not(live))
    def _():
        o_ref[...] = jnp.zeros(o_ref.shape, o_ref.dtype)

    @pl.when(live)
    def _():
        y = jnp.dot(x_ref[...], wd_ref[...], preferred_element_type=_F32) + bd_ref[...]
        o_ref[...] = y * gate_ref[...]


def _work_w(w, we, wjw, wr, wjo, lv):
    return (we[w], 0, wjw[w])


def _work_x(w, we, wjw, wr, wjo, lv):
    return (wr[w], 0)


def _work_o(w, we, wjw, wr, wjo, lv):
    return (wr[w], wjo[w])


def _moe_up(work, xs, wg, wl, bg, bl, bm, tn):
    p = xs.shape[0]
    n_work = work[0].shape[0]
    wspec = pl.BlockSpec((None, D_MODEL, tn), _work_w)
    bspec = pl.BlockSpec((None, 1, tn), _work_w)
    blk = _nbytes((bm, D_MODEL), _BF16) + 2 * _nbytes((D_MODEL, tn), _BF16) + _nbytes((bm, tn), _BF16)
    return pl.pallas_call(
        _up_body,
        grid_spec=pltpu.PrefetchScalarGridSpec(
            num_scalar_prefetch=len(work),
            grid=(n_work,),
            in_specs=[pl.BlockSpec((bm, D_MODEL), _work_x), wspec, wspec, bspec, bspec],
            out_specs=pl.BlockSpec((bm, tn), _work_o),
        ),
        out_shape=jax.ShapeDtypeStruct((p, D_FF), _BF16),
        compiler_params=_params(("arbitrary",), blk, 6 * _nbytes((bm, tn), _F32)),
        name="moe_up",
    )(*work, xs, wg, wl, bg, bl)


def _moe_down(work, hact, wd, bd, slot_gate, bm, tn):
    p = hact.shape[0]
    n_work = work[0].shape[0]
    blk = (_nbytes((bm, D_FF), _BF16) + _nbytes((D_FF, tn), _BF16) + _nbytes((bm, tn), _F32)
           + _nbytes((bm, LANE), _F32))
    return pl.pallas_call(
        _down_body,
        grid_spec=pltpu.PrefetchScalarGridSpec(
            num_scalar_prefetch=len(work),
            grid=(n_work,),
            in_specs=[pl.BlockSpec((bm, D_FF), _work_x),
                      pl.BlockSpec((None, D_FF, tn), _work_w),
                      pl.BlockSpec((None, 1, tn), _work_w),
                      pl.BlockSpec((bm, 1), _work_x)],
            out_specs=pl.BlockSpec((bm, tn), _work_o),
        ),
        out_shape=jax.ShapeDtypeStruct((p, D_MODEL), _F32),
        compiler_params=_params(("arbitrary",), blk, 4 * _nbytes((bm, tn), _F32)),
        name="moe_down",
    )(*work, hact, wd, bd, slot_gate)


def _combine_body(pos_ref, yd_hbm, x1_ref, g_ref, b_ref, o_ref, buf, sem, *, tb):
    def issue(t, carry):
        for k_sel in range(TOP_K):
            _row_copy(yd_hbm, pos_ref[0, 0, t * TOP_K + k_sel], buf.at[k_sel], t, sem).start()
        return carry

    def drain(t, carry):
        for k_sel in range(TOP_K):
            _row_copy(yd_hbm, 0, buf.at[k_sel], t, sem).wait()
        return carry

    lax.fori_loop(0, tb, issue, 0)
    lax.fori_loop(0, tb, drain, 0)
    ffn = (buf[0] + buf[1]) + (buf[2] + buf[3])
    o_ref[...] = _layer_norm(DEEPNORM_ALPHA * x1_ref[...] + ffn, g_ref[...], b_ref[...])


def _combine(yd, pos, x1, ln_g, ln_b, tb):
    n = x1.shape[0]
    nb = n // tb
    pos3 = pos.reshape(nb, 1, tb * TOP_K)
    row = pl.BlockSpec((tb, D_MODEL), lambda i: (i, 0))
    vec = pl.BlockSpec((1, D_MODEL), lambda i: (0, 0))
    blk = 2 * _nbytes((tb, D_MODEL), _F32)
    return pl.pallas_call(
        functools.partial(_combine_body, tb=tb),
        grid=(nb,),
        in_specs=[pl.BlockSpec((1, 1, tb * TOP_K), lambda i: (i, 0, 0), memory_space=pltpu.SMEM),
                  pl.BlockSpec(memory_space=pl.ANY), row, vec, vec],
        out_specs=row,
        out_shape=jax.ShapeDtypeStruct((n, D_MODEL), _F32),
        scratch_shapes=[pltpu.VMEM((TOP_K, tb, D_MODEL), _F32), pltpu.SemaphoreType.DMA(())],
        compiler_params=_params(("arbitrary",), blk, (TOP_K + 4) * _nbytes((tb, D_MODEL), _F32)),
        name="moe_combine_ln2",
    )(pos3, yd, x1, ln_g, ln_b)


def _routing(top_idx, gate, bm, n_col_tiles):
    n_tok = top_idx.shape[0]
    n_assign = n_tok * TOP_K
    n_blocks_max = (n_assign + N_EXPERTS * (bm - 1) + bm - 1) // bm
    p = n_blocks_max * bm
    flat_e = top_idx.reshape(-1)
    flat_tok = jnp.repeat(jnp.arange(n_tok, dtype=jnp.int32), TOP_K)
    onehot = (flat_e[:, None] == jnp.arange(N_EXPERTS, dtype=jnp.int32)[None, :]).astype(jnp.int32)
    csum = jnp.cumsum(onehot, axis=0)
    rank = jnp.take_along_axis(csum, flat_e[:, None], axis=1)[:, 0] - 1
    counts = csum[-1]
    nblk_e = (counts + bm - 1) // bm
    blk_end = jnp.cumsum(nblk_e)
    blk_start = blk_end - nblk_e
    dest = (blk_start[flat_e] * bm + rank).astype(jnp.int32)
    slot_tok = jnp.zeros((p,), jnp.int32).at[dest].set(flat_tok, unique_indices=True)
    slot_gate = jnp.zeros((p,), _F32).at[dest].set(gate.reshape(-1), unique_indices=True)
    n_live_blocks = blk_end[-1].astype(jnp.int32)
    n_work = n_blocks_max * n_col_tiles
    n_live_work = n_live_blocks * n_col_tiles
    w_all = jnp.arange(n_work, dtype=jnp.int32)
    live = w_all < n_live_work
    w = jnp.minimum(w_all, n_live_work - 1)
    work_end = blk_end * n_col_tiles
    we = jnp.minimum(jnp.searchsorted(work_end, w, side='right'), N_EXPERTS - 1).astype(jnp.int32)
    local = w - blk_start[we] * n_col_tiles
    nb_e = jnp.maximum(nblk_e[we], 1)
    wjw = (local // nb_e).astype(jnp.int32)
    dead = w_all - n_live_work
    wr = jnp.where(live, blk_start[we] + local % nb_e, n_live_blocks + dead // n_col_tiles).astype(jnp.int32)
    wjo = jnp.where(live, wjw, dead % n_col_tiles).astype(jnp.int32)
    work = (we, wjw, wr, wjo, live.astype(jnp.int32))
    return slot_tok, slot_gate, dest.reshape(n_tok, TOP_K), n_live_blocks.reshape(1), work


def kernel(x_prompt, x_sample, mem_prompt, cache_mem_k, cache_mem_v, state_gla, state_pool, w_in, w_fg, b_fg, gla_norm_g, pool_w, pool_scale, w_mem_k, w_mem_v, w_br_gla, w_br_pool, w_br_mem, b_gate, w_out, ln1_g, ln1_b, w_router, b_router, w_up, b_up, w_down, b_down, ln2_g, ln2_b):
    assert w_in.shape[0] == DEPTH == 1
    bp, tp, _ = x_prompt.shape
    bs, ts, _ = x_sample.shape
    n_p, n_s = bp * tp, bs * ts
    n_tok = n_p + n_s
    m_tok = mem_prompt.shape[1]

    x_all = jnp.concatenate([x_prompt.reshape(n_p, D_MODEL), x_sample.reshape(n_s, D_MODEL)], axis=0)
    x_bf = x_all.astype(_BF16)
    w_in0 = w_in[0]
    w_main = jnp.concatenate([w_in0[:, :FG_COL0], w_in0[:, FG_COL0 + GLA_GATE_RANK:]], axis=1).astype(_BF16)
    w_fgl = jnp.pad(w_in0[:, FG_COL0:FG_COL0 + GLA_GATE_RANK], ((0, 0), (0, FG_PAD - GLA_GATE_RANK))).astype(_BF16)
    wfg_pad = jnp.pad(w_fg[0], ((0, FG_PAD - GLA_GATE_RANK), (0, 0)))
    b_fg2 = b_fg[0].reshape(1, GLA_DK)
    gla_g = gla_norm_g[0].reshape(1, GLA_DV)

    z = _matmul(x_bf, w_main, 1024, 1024, name="in_proj")
    fgl = _matmul(x_bf, w_fgl, 1024, FG_PAD, name="in_proj_fg")

    mem_bf = mem_prompt.reshape(bp * m_tok, D_MODEL).astype(_BF16)
    mk = _matmul(mem_bf, w_mem_k[0].astype(_BF16), 512, 512, name="mem_k")
    mv = _matmul(mem_bf, w_mem_v[0].astype(_BF16), 512, 512, name="mem_v")
    mk3 = mk.reshape(bp, m_tok, MEM_WIDTH)
    mv3 = mv.reshape(bp, m_tok, MEM_WIDTH)

    s0_p = jnp.zeros((bp, GLA_HEADS, GLA_HEAD_DK, GLA_HEAD_DV), _F32)
    oa_p, gla_p = _gla(z, fgl, wfg_pad, b_fg2, gla_g, s0_p, row0=0, bsz=bp, t_len=tp, tb=256, c=GLA_SUB)
    oa_s, gla_s = _gla(z, fgl, wfg_pad, b_fg2, gla_g, state_gla[0], row0=n_p, bsz=bs, t_len=ts, tb=ts, c=ts)

    scale2 = pool_scale[0].reshape(1, POOL_WIDTH)
    ob_p = _pool(z, z, pool_w[0], scale2, row0=0, bsz=bp, t_len=tp, tb=512, pos0=0, halo_from_z=True)
    halo_s = jnp.pad(state_pool[0], ((0, 0), (HALO - POOL_BUF, 0), (0, 0)))
    ob_s = _pool(z, halo_s, pool_w[0], scale2, row0=n_p, bsz=bs, t_len=ts, tb=ts, pos0=PAST_LEN, halo_from_z=False)

    oc_p = _memattn(z, mk3, mv3, row0=0, bsz=bp, t_len=tp, tb=512)
    oc_s = _memattn(z, cache_mem_k[0].reshape(bs, m_tok, MEM_WIDTH), cache_mem_v[0].reshape(bs, m_tok, MEM_WIDTH),
                    row0=n_p, bsz=bs, t_len=ts, tb=ts)

    br_a = jnp.concatenate([oa_p, oa_s], axis=0).astype(_BF16)
    br_b = jnp.concatenate([ob_p, ob_s], axis=0).astype(_BF16)
    br_c = jnp.concatenate([oc_p, oc_s], axis=0).astype(_BF16)

    h = _merge(br_a, br_b, br_c, w_br_gla[0].astype(_BF16), w_br_pool[0].astype(_BF16), w_br_mem[0].astype(_BF16),
               z, b_gate[0].reshape(1, N_BRANCHES * D_MODEL), 1024, 512)
    t_out = _matmul(h, w_out[0].astype(_BF16), 1024, 1024, name="out_proj")
    x1, idx_pad, gate_pad = _ln_router(x_all, t_out, ln1_g[0].reshape(1, D_MODEL), ln1_b[0].reshape(1, D_MODEL),
                                       w_router[0], b_router[0].reshape(1, N_EXPERTS), 256)
    top_idx = idx_pad[:, :TOP_K]
    gate = gate_pad[:, :TOP_K]

    n_col_tiles = D_FF // MOE_TN
    slot_tok, slot_gate, pos, n_live_blocks, work = _routing(top_idx, gate, MOE_BM, n_col_tiles)
    xs = _dispatch(x1, slot_tok, n_live_blocks, MOE_BM)
    w_up0 = w_up[0]
    wg = w_up0[:, :, 0::2].astype(_BF16)
    wl = w_up0[:, :, 1::2].astype(_BF16)
    bg = b_up[0][:, 0::2].reshape(N_EXPERTS, 1, D_FF)
    bl = b_up[0][:, 1::2].reshape(N_EXPERTS, 1, D_FF)
    hact = _moe_up(work, xs, wg, wl, bg, bl, MOE_BM, MOE_TN)
    yd = _moe_down(work, hact, w_down[0].astype(_BF16), b_down[0].reshape(N_EXPERTS, 1, D_MODEL),
                   slot_gate.reshape(-1, 1), MOE_BM, MOE_TN)
    y = _combine(yd, pos, x1, ln2_g[0].reshape(1, D_MODEL), ln2_b[0].reshape(1, D_MODEL), 128)

    y_p = y[:n_p].reshape(bp, tp, D_MODEL)
    y_s = y[n_p:].reshape(bs, ts, D_MODEL)
    u_p = z[:n_p, COL_U:COL_U + POOL_WIDTH].reshape(bp, tp, POOL_WIDTH)
    u_s = z[n_p:, COL_U:COL_U + POOL_WIDTH].reshape(bs, ts, POOL_WIDTH)
    pool_p = u_p[:, tp - POOL_BUF:]
    pool_s = jnp.concatenate([state_pool[0], u_s], axis=1)[:, -POOL_BUF:]
    mk_out = mk.reshape(1, bp, m_tok, MEM_HEADS, MEM_HEAD_DIM)
    mv_out = mv.reshape(1, bp, m_tok, MEM_HEADS, MEM_HEAD_DIM)
    return (y_p, y_s, mk_out, mv_out, gla_p[None], pool_p[None], gla_s[None], pool_s[None])
```

```python
import functools
import math

import jax
import jax.numpy as jnp
from jax import lax
from jax.experimental import pallas as pl
from jax.experimental.pallas import tpu as pltpu

D_MODEL = 4096
DEPTH = 1
GLA_HEADS = 4
GLA_DV = D_MODEL // 2
GLA_DK = GLA_DV // 2
GLA_HEAD_DK = GLA_DK // GLA_HEADS
GLA_HEAD_DV = GLA_DV // GLA_HEADS
GLA_GATE_RANK = 16
GLA_GATE_TAU = 16.0
POOL_WIDTH = D_MODEL // 4
POOL_WINDOWS = (2, 4, 8, 16)
POOL_GROUP_DIM = POOL_WIDTH // len(POOL_WINDOWS)
POOL_BUF = max(POOL_WINDOWS) - 1
MEM_HEADS = 4
MEM_WIDTH = D_MODEL // 4
MEM_HEAD_DIM = MEM_WIDTH // MEM_HEADS
N_BRANCHES = 3
N_EXPERTS = 32
TOP_K = 4
D_FF = D_MODEL
SWIGLU_LIMIT = 7.0
SWIGLU_ALPHA = 1.702
NORM_EPS = 1e-5
DEEPNORM_ALPHA = (2.0 * DEPTH) ** 0.25
PAST_LEN = 16384

COL_Q = 0
COL_K = GLA_DK
COL_V = 2 * GLA_DK
COL_R = COL_V + GLA_DV
COL_U = COL_R + GLA_DV
COL_MQ = COL_U + POOL_WIDTH
COL_G = COL_MQ + MEM_WIDTH
Z_WIDTH = COL_G + N_BRANCHES * D_MODEL
FG_COL0 = COL_R + GLA_DV
FG_PAD = 128

V7X_VMEM_BYTES = 64 * 1024 * 1024
V7X_VMEM_REQUEST_CAP = 56 * 1024 * 1024
LANE = 128
LANE_BITS = 7
SUBLANE = 8

GLA_SUB = 16
MOE_BM = 256
MOE_UP_TN = 512
MOE_DOWN_TN = 1024
UP_CONV_ROWS = 256
DOWN_CONV_ROWS = 256
HALO = 16

_HI = lax.Precision.HIGHEST
_F32 = jnp.float32
_BF16 = jnp.bfloat16


def _nbytes(shape, dtype):
    return math.prod(shape) * jnp.dtype(dtype).itemsize


def _params(semantics, block_bytes, extra_bytes=0):
    need = 2 * block_bytes + extra_bytes + (4 << 20)
    return pltpu.CompilerParams(dimension_semantics=semantics,
                                vmem_limit_bytes=int(min(max(need, 16 << 20), V7X_VMEM_REQUEST_CAP)))


def _mm_body(x_ref, w_ref, o_ref):
    o_ref[...] = jnp.dot(x_ref[...], w_ref[...], preferred_element_type=_F32).astype(o_ref.dtype)


def _matmul(x, w, bm, bn, out_dtype=_F32, name="matmul"):
    m, k = x.shape
    _, n = w.shape
    assert m % bm == 0 and n % bn == 0
    blk = _nbytes((bm, k), x.dtype) + _nbytes((k, bn), w.dtype) + _nbytes((bm, bn), out_dtype)
    return pl.pallas_call(
        _mm_body,
        grid=(m // bm, n // bn),
        in_specs=[pl.BlockSpec((bm, k), lambda i, j: (i, 0)),
                  pl.BlockSpec((k, bn), lambda i, j: (0, j))],
        out_specs=pl.BlockSpec((bm, bn), lambda i, j: (i, j)),
        out_shape=jax.ShapeDtypeStruct((m, n), out_dtype),
        compiler_params=_params(("parallel", "arbitrary"), blk, _nbytes((bm, bn), _F32)),
        name=name,
    )(x, w)


def _log_sigmoid(x):
    return jnp.minimum(x, 0.0) - jnp.log(1.0 + jnp.exp(-jnp.abs(x)))


def _gla_body(q_ref, k_ref, v_ref, r_ref, fg_ref, wfg_ref, bfg_ref, g_ref, s0_ref,
              o_ref, s_out_ref,
              st_scr, q_scr, cum_scr, qd_scr, kd_scr, tot_scr, o_scr, *, tb, c):
    t_blk = pl.program_id(2)
    n_sub = tb // c
    shift = int(math.log2(c))

    @pl.when(t_blk == 0)
    def _():
        st_scr[...] = s0_ref[...].T

    fg = jnp.dot(fg_ref[...], wfg_ref[...], precision=_HI, preferred_element_type=_F32) + bfg_ref[...]
    la = _log_sigmoid(fg) / GLA_GATE_TAU
    row = lax.broadcasted_iota(jnp.int32, (tb, tb), 0)
    col = lax.broadcasted_iota(jnp.int32, (tb, tb), 1)
    same = jnp.right_shift(row, shift) == jnp.right_shift(col, shift)
    incl = jnp.where(same & (col <= row), 1.0, 0.0).astype(_F32)
    cum = jnp.dot(incl, la, precision=_HI, preferred_element_type=_F32)
    tot = jnp.dot(same.astype(_F32), la, precision=_HI, preferred_element_type=_F32)
    q = q_ref[...] * (GLA_HEAD_DK ** -0.5)
    q_scr[...] = q
    cum_scr[...] = cum
    tot_scr[...] = tot
    qd_scr[...] = q * jnp.exp(cum)
    kd_scr[...] = k_ref[...] * jnp.exp(tot - cum)

    def sub_chunk(s, carry):
        r0 = pl.multiple_of(s * c, c)
        rows = pl.ds(r0, c)
        q_s = q_scr[rows, :]
        k_s = k_ref[rows, :]
        v_s = v_ref[rows, :]
        cum_s = cum_scr[rows, :]
        j_idx = lax.broadcasted_iota(jnp.int32, (c, 1), 0)
        outs = []
        for i in range(c):
            rel = cum_s[i:i + 1, :] - cum_s
            decay = jnp.exp(jnp.where(j_idx <= i, rel, -jnp.inf))
            score = jnp.sum(decay * k_s * q_s[i:i + 1, :], axis=1, keepdims=True)
            outs.append(jnp.sum(score * v_s, axis=0, keepdims=True))
        o_diag = jnp.concatenate(outs, axis=0)
        st = st_scr[...]
        o_state = lax.dot_general(qd_scr[rows, :].astype(_BF16), st.astype(_BF16),
                                  (((1,), (1,)), ((), ())), preferred_element_type=_F32)
        o_scr[rows, :] = o_diag + o_state
        upd = lax.dot_general(v_s.astype(_BF16), kd_scr[rows, :].astype(_BF16),
                              (((0,), (0,)), ((), ())), preferred_element_type=_F32)
        st_scr[...] = st * jnp.exp(tot_scr[pl.ds(r0, 1), :]) + upd
        return carry

    lax.fori_loop(0, n_sub, sub_chunk, 0)

    o = o_scr[...]
    o = o * lax.rsqrt(jnp.mean(o * o, axis=-1, keepdims=True) + NORM_EPS) * g_ref[...]
    r = r_ref[...]
    o_ref[...] = (o * (r * jax.nn.sigmoid(r))).astype(o_ref.dtype)

    @pl.when(t_blk == pl.num_programs(2) - 1)
    def _():
        s_out_ref[...] = st_scr[...].T


def _gla(z, fgl, wfg_pad, b_fg, gla_g, s0, *, row0, bsz, t_len, tb, c, out_dtype):
    nt = t_len // tb
    rb0 = row0 // tb
    dk, dv = GLA_HEAD_DK, GLA_HEAD_DV

    def rows(b, h, t):
        return rb0 + b * nt + t

    kernel = functools.partial(_gla_body, tb=tb, c=c)
    blk = (2 * _nbytes((tb, dk), _F32) + 2 * _nbytes((tb, dv), _F32) + _nbytes((tb, FG_PAD), _F32)
           + _nbytes((FG_PAD, dk), _F32) + 2 * _nbytes((dk, dv), _F32) + _nbytes((tb, dv), _F32))
    scr = _nbytes((dv, dk), _F32) + 5 * _nbytes((tb, dk), _F32) + _nbytes((tb, dv), _F32)
    return pl.pallas_call(
        kernel,
        grid=(bsz, GLA_HEADS, nt),
        in_specs=[
            pl.BlockSpec((tb, dk), lambda b, h, t: (rows(b, h, t), COL_Q // dk + h)),
            pl.BlockSpec((tb, dk), lambda b, h, t: (rows(b, h, t), COL_K // dk + h)),
            pl.BlockSpec((tb, dv), lambda b, h, t: (rows(b, h, t), COL_V // dv + h)),
            pl.BlockSpec((tb, dv), lambda b, h, t: (rows(b, h, t), COL_R // dv + h)),
            pl.BlockSpec((tb, FG_PAD), lambda b, h, t: (rows(b, h, t), 0)),
            pl.BlockSpec((FG_PAD, dk), lambda b, h, t: (0, h)),
            pl.BlockSpec((1, dk), lambda b, h, t: (0, h)),
            pl.BlockSpec((1, dv), lambda b, h, t: (0, h)),
            pl.BlockSpec((None, None, None, dk, dv), lambda b, h, t: (0, b, h, 0, 0)),
        ],
        out_specs=[
            pl.BlockSpec((tb, dv), lambda b, h, t: (b * nt + t, h)),
            pl.BlockSpec((None, None, None, dk, dv), lambda b, h, t: (0, b, h, 0, 0)),
        ],
        out_shape=[jax.ShapeDtypeStruct((bsz * t_len, GLA_DV), out_dtype),
                   jax.ShapeDtypeStruct((1, bsz, GLA_HEADS, dk, dv), _F32)],
        scratch_shapes=[pltpu.VMEM((dv, dk), _F32), pltpu.VMEM((tb, dk), _F32), pltpu.VMEM((tb, dk), _F32),
                        pltpu.VMEM((tb, dk), _F32), pltpu.VMEM((tb, dk), _F32), pltpu.VMEM((tb, dk), _F32),
                        pltpu.VMEM((tb, dv), _F32)],
        compiler_params=_params(("parallel", "parallel", "arbitrary"), blk, scr + (8 << 20)),
        name=f"gla_t{tb}",
    )(z, z, z, z, fgl, wfg_pad, b_fg, gla_g, s0)


def _pool_body(u_ref, halo_ref, pw_ref, scale_ref, o_ref, ext_scr, *, tb, pos0, first_block_has_no_history):
    t_blk = pl.program_id(1)
    halo = halo_ref[...]
    if first_block_has_no_history:
        halo = jnp.where(t_blk > 0, halo, 0.0)
    ext_scr[0:HALO, :] = halo
    ext_scr[HALO:HALO + tb, :] = u_ref[...]
    pos = pos0 + t_blk * tb + lax.broadcasted_iota(jnp.int32, (tb, 1), 0)
    gd = POOL_GROUP_DIM
    for gi, win in enumerate(POOL_WINDOWS):
        cols = slice(gi * gd, (gi + 1) * gd)
        tok = ext_scr[HALO:HALO + tb, cols]
        acc = tok
        for s in range(1, win):
            acc = acc + ext_scr[HALO - s:HALO - s + tb, cols]
        cnt = jnp.minimum(pos + 1, win).astype(_F32)
        diff = acc / cnt - tok
        mixed = jnp.dot(diff.astype(_BF16), pw_ref[gi].astype(_BF16), preferred_element_type=_F32)
        o_ref[:, cols] = (mixed * scale_ref[:, cols]).astype(o_ref.dtype)


def _pool(z, halo, pool_w, pool_scale, *, row0, bsz, t_len, tb, pos0, halo_from_z, out_dtype):
    nt = t_len // tb
    rb0 = row0 // tb
    ucol = COL_U // POOL_WIDTH
    kernel = functools.partial(_pool_body, tb=tb, pos0=pos0, first_block_has_no_history=halo_from_z)
    if halo_from_z:
        hb0 = row0 // HALO
        per = tb // HALO
        halo_spec = pl.BlockSpec((HALO, POOL_WIDTH),
                                 lambda b, t: (jnp.maximum(hb0 + (b * nt + t) * per - 1, 0), ucol))
    else:
        halo_spec = pl.BlockSpec((None, HALO, POOL_WIDTH), lambda b, t: (b, 0, 0))
    blk = (2 * _nbytes((tb, POOL_WIDTH), _F32) + _nbytes((HALO, POOL_WIDTH), _F32)
           + _nbytes(pool_w.shape, _F32))
    return pl.pallas_call(
        kernel,
        grid=(bsz, nt),
        in_specs=[
            pl.BlockSpec((tb, POOL_WIDTH), lambda b, t: (rb0 + b * nt + t, ucol)),
            halo_spec,
            pl.BlockSpec(pool_w.shape, lambda b, t: (0, 0, 0)),
            pl.BlockSpec((1, POOL_WIDTH), lambda b, t: (0, 0)),
        ],
        out_specs=pl.BlockSpec((tb, POOL_WIDTH), lambda b, t: (b * nt + t, 0)),
        out_shape=jax.ShapeDtypeStruct((bsz * t_len, POOL_WIDTH), out_dtype),
        scratch_shapes=[pltpu.VMEM((HALO + tb, POOL_WIDTH), _F32)],
        compiler_params=_params(("parallel", "arbitrary"), blk, 4 * _nbytes((tb, POOL_WIDTH), _F32)),
        name=f"pool_t{tb}",
    )(z, halo, pool_w, pool_scale)


def _memattn_body(q_ref, k_ref, v_ref, o_ref):
    hd = MEM_HEAD_DIM
    for h in range(MEM_HEADS):
        cols = slice(h * hd, (h + 1) * hd)
        q = (q_ref[:, cols] * (hd ** -0.5)).astype(_BF16)
        k = k_ref[:, h, :].astype(_BF16)
        s = lax.dot_general(q, k, (((1,), (1,)), ((), ())), preferred_element_type=_F32)
        p = jnp.exp(s - jnp.max(s, axis=-1, keepdims=True))
        p = p / jnp.sum(p, axis=-1, keepdims=True)
        o = jnp.dot(p.astype(_BF16), v_ref[:, h, :].astype(_BF16), preferred_element_type=_F32)
        o_ref[:, cols] = o.astype(o_ref.dtype)


def _memattn(z, mem_k, mem_v, *, row0, bsz, t_len, tb, out_dtype):
    nt = t_len // tb
    rb0 = row0 // tb
    m_tok = mem_k.shape[2]
    mem_block = (None, None, m_tok, MEM_HEADS, MEM_HEAD_DIM)
    sublane_padded = _nbytes((m_tok, SUBLANE, MEM_HEAD_DIM), _F32)
    blk = 2 * _nbytes((tb, MEM_WIDTH), _F32) + 2 * sublane_padded
    return pl.pallas_call(
        _memattn_body,
        grid=(bsz, nt),
        in_specs=[
            pl.BlockSpec((tb, MEM_WIDTH), lambda b, t: (rb0 + b * nt + t, COL_MQ // MEM_WIDTH)),
            pl.BlockSpec(mem_block, lambda b, t: (0, b, 0, 0, 0)),
            pl.BlockSpec(mem_block, lambda b, t: (0, b, 0, 0, 0)),
        ],
        out_specs=pl.BlockSpec((tb, MEM_WIDTH), lambda b, t: (b * nt + t, 0)),
        out_shape=jax.ShapeDtypeStruct((bsz * t_len, MEM_WIDTH), out_dtype),
        compiler_params=_params(("parallel", "arbitrary"), blk, 6 * _nbytes((tb, m_tok), _F32) + sublane_padded),
        name=f"memattn_t{tb}",
    )(z, mem_k, mem_v)


def _merge_body(a_ref, b_ref, c_ref, wa_ref, wb_ref, wc_ref, g0_ref, g1_ref, g2_ref,
                bg0_ref, bg1_ref, bg2_ref, o_ref):
    def gated(x_ref, w_ref, g_ref, bg_ref):
        br = jnp.dot(x_ref[...], w_ref[...], preferred_element_type=_F32)
        return jax.nn.sigmoid(g_ref[...] + bg_ref[...]) * br

    h = gated(a_ref, wa_ref, g0_ref, bg0_ref)
    h = h + gated(b_ref, wb_ref, g1_ref, bg1_ref)
    h = h + gated(c_ref, wc_ref, g2_ref, bg2_ref)
    o_ref[...] = h.astype(o_ref.dtype)


def _merge(br_a, br_b, br_c, wa, wb, wc, z, b_gate, bm, bn):
    n = br_a.shape[0]
    nj = D_MODEL // bn
    g0 = COL_G // bn

    def gspec(i_br):
        return pl.BlockSpec((bm, bn), lambda i, j: (i, g0 + i_br * nj + j))

    def bspec(i_br):
        return pl.BlockSpec((1, bn), lambda i, j: (0, i_br * nj + j))

    def xspec(width):
        return pl.BlockSpec((bm, width), lambda i, j: (i, 0))

    def wspec(width):
        return pl.BlockSpec((width, bn), lambda i, j: (0, j))

    blk = (_nbytes((bm, D_MODEL), _BF16) + _nbytes((D_MODEL, bn), _BF16) + 3 * _nbytes((bm, bn), _F32)
           + _nbytes((bm, bn), _BF16))
    return pl.pallas_call(
        _merge_body,
        grid=(n // bm, nj),
        in_specs=[xspec(GLA_DV), xspec(POOL_WIDTH), xspec(MEM_WIDTH),
                  wspec(GLA_DV), wspec(POOL_WIDTH), wspec(MEM_WIDTH),
                  gspec(0), gspec(1), gspec(2), bspec(0), bspec(1), bspec(2)],
        out_specs=pl.BlockSpec((bm, bn), lambda i, j: (i, j)),
        out_shape=jax.ShapeDtypeStruct((n, D_MODEL), _BF16),
        compiler_params=_params(("parallel", "arbitrary"), blk, 4 * _nbytes((bm, bn), _F32)),
        name="merge",
    )(br_a, br_b, br_c, wa, wb, wc, z, z, z, b_gate, b_gate, b_gate)


def _layer_norm(v, g, b):
    mu = jnp.mean(v, axis=-1, keepdims=True)
    d = v - mu
    var = jnp.mean(d * d, axis=-1, keepdims=True)
    return d * lax.rsqrt(var + NORM_EPS) * g + b


def _ln_router_body(x_ref, t_ref, g_ref, b_ref, wr_ref, br_ref, x1_ref, idx_ref, gate_ref):
    x1 = _layer_norm(DEEPNORM_ALPHA * x_ref[...] + t_ref[...], g_ref[...], b_ref[...])
    x1_ref[...] = x1
    logits = jnp.dot(x1, wr_ref[...], precision=_HI, preferred_element_type=_F32) + br_ref[...]
    bm = logits.shape[0]
    e_idx = lax.broadcasted_iota(jnp.int32, logits.shape, 1)
    lane = lax.broadcasted_iota(jnp.int32, (bm, LANE), 1)
    vals, idxs = [], []
    for _ in range(TOP_K):
        m = jnp.max(logits, axis=-1, keepdims=True)
        first = jnp.min(jnp.where(logits == m, e_idx, N_EXPERTS), axis=-1, keepdims=True)
        vals.append(m)
        idxs.append(first)
        logits = jnp.where(e_idx == first, -jnp.inf, logits)
    exps = [jnp.exp(v - vals[0]) for v in vals]
    denom = exps[0] + exps[1] + exps[2] + exps[3]
    idx_out = jnp.zeros((bm, LANE), jnp.int32)
    gate_out = jnp.zeros((bm, LANE), _F32)
    for k_sel in range(TOP_K):
        idx_out = jnp.where(lane == k_sel, idxs[k_sel], idx_out)
        gate_out = jnp.where(lane == k_sel, exps[k_sel] / denom, gate_out)
    idx_ref[...] = idx_out
    gate_ref[...] = gate_out


def _ln_router(x, t, ln_g, ln_b, w_router, b_router, bm):
    n = x.shape[0]
    row = pl.BlockSpec((bm, D_MODEL), lambda i: (i, 0))
    vec = pl.BlockSpec((1, D_MODEL), lambda i: (0, 0))
    small = pl.BlockSpec((bm, LANE), lambda i: (i, 0))
    blk = 3 * _nbytes((bm, D_MODEL), _F32) + _nbytes((D_MODEL, LANE), _F32)
    return pl.pallas_call(
        _ln_router_body,
        grid=(n // bm,),
        in_specs=[row, row, vec, vec,
                  pl.BlockSpec((D_MODEL, N_EXPERTS), lambda i: (0, 0)),
                  pl.BlockSpec((1, N_EXPERTS), lambda i: (0, 0))],
        out_specs=[row, small, small],
        out_shape=[jax.ShapeDtypeStruct((n, D_MODEL), _F32),
                   jax.ShapeDtypeStruct((n, LANE), jnp.int32), jax.ShapeDtypeStruct((n, LANE), _F32)],
        compiler_params=_params(("parallel",), blk, 8 * _nbytes((bm, D_MODEL), _F32)),
        name="ln1_router",
    )(x, t, ln_g, ln_b, w_router, b_router)


def _row_copy(src_hbm, row, dst, r, sem):
    return pltpu.make_async_copy(src_hbm.at[pl.ds(row, 1), :], dst.at[pl.ds(r, 1), :], sem)


def _dispatch_body(nlive_ref, tok_ref, tok_next_ref, x_hbm, o_ref, buf, sems, *, gb):
    i = pl.program_id(0)
    n_live = nlive_ref[0]
    slot = i % 2

    def issue(tok, s):
        def one(r, carry):
            _row_copy(x_hbm, tok[0, 0, r], buf.at[s], r, sems.at[s]).start()
            return carry
        lax.fori_loop(0, gb, one, 0)

    @pl.when((i == 0) & (n_live > 0))
    def _():
        issue(tok_ref, 0)

    @pl.when(i + 1 < jnp.minimum(n_live, pl.num_programs(0)))
    def _():
        issue(tok_next_ref, 1 - slot)

    @pl.when(i < n_live)
    def _():
        def drain(r, carry):
            _row_copy(x_hbm, 0, buf.at[slot], r, sems.at[slot]).wait()
            return carry

        lax.fori_loop(0, gb, drain, 0)
        o_ref[...] = buf[slot].astype(o_ref.dtype)

    @pl.when(i >= n_live)
    def _():
        o_ref[...] = jnp.zeros(o_ref.shape, o_ref.dtype)


def _dispatch(x1, slot_tok, n_live_blocks, gb):
    p = slot_tok.shape[0]
    nb = p // gb
    tok3 = slot_tok.reshape(nb, 1, gb)
    blk = _nbytes((gb, D_MODEL), _BF16)
    return pl.pallas_call(
        functools.partial(_dispatch_body, gb=gb),
        grid_spec=pltpu.PrefetchScalarGridSpec(
            num_scalar_prefetch=1,
            grid=(nb,),
            in_specs=[pl.BlockSpec((1, 1, gb), lambda i, nl: (i, 0, 0), memory_space=pltpu.SMEM),
                      pl.BlockSpec((1, 1, gb), lambda i, nl: (jnp.minimum(i + 1, nb - 1), 0, 0),
                                   memory_space=pltpu.SMEM),
                      pl.BlockSpec(memory_space=pl.ANY)],
            out_specs=pl.BlockSpec((gb, D_MODEL), lambda i, nl: (i, 0)),
            scratch_shapes=[pltpu.VMEM((2, gb, D_MODEL), _F32), pltpu.SemaphoreType.DMA((2,))],
        ),
        out_shape=jax.ShapeDtypeStruct((p, D_MODEL), _BF16),
        compiler_params=_params(("arbitrary",), blk, 3 * _nbytes((gb, D_MODEL), _F32)),
        name="moe_dispatch",
    )(n_live_blocks, tok3, tok3, x1)


def _pair_order(p):
    return jnp.where((p & 1) == 0, p >> 1, (LANE // 2) + (p >> 1))


def _up_body(we_ref, wjw_ref, wr_ref, wjo_ref, live_ref, fresh_ref, x_ref, w_ref, b_ref, o_ref, w_scr, *, tn):
    step = pl.program_id(0)
    live = live_ref[step] == 1
    wide = 2 * LANE

    @pl.when(fresh_ref[step] == 1)
    def _():
        def convert(rc, carry):
            rows = pl.ds(pl.multiple_of(rc * UP_CONV_ROWS, UP_CONV_ROWS), UP_CONV_ROWS)
            w_scr[rows, :] = w_ref[rows, :].astype(_BF16)
            return carry

        lax.fori_loop(0, D_MODEL // UP_CONV_ROWS, convert, 0)

    @pl.when(jnp.logical_not(live))
    def _():
        o_ref[...] = jnp.zeros(o_ref.shape, o_ref.dtype)

    @pl.when(live)
    def _():
        h = jnp.dot(x_ref[...], w_scr[...], preferred_element_type=_F32) + b_ref[...]
        even = (lax.broadcasted_iota(jnp.int32, (h.shape[0], LANE), 1) & 1) == 0
        p_in = lax.broadcasted_iota(jnp.int32, (wide, wide), 0)
        p_out = lax.broadcasted_iota(jnp.int32, (wide, wide), 1)
        same_grp = (p_in >> LANE_BITS) == (p_out >> LANE_BITS)
        unshuffle = jnp.where(same_grp & (_pair_order(p_in & (LANE - 1)) == (p_out & (LANE - 1))), 1.0, 0.0)
        unshuffle = unshuffle.astype(_BF16)
        for m2 in range(tn // wide):
            acts = []
            for m in (2 * m2, 2 * m2 + 1):
                a = h[:, (2 * m) * LANE:(2 * m + 1) * LANE]
                b = h[:, (2 * m + 1) * LANE:(2 * m + 2) * LANE]
                h_glu = jnp.where(even, a, pltpu.roll(b, 1, axis=1))
                h_lin = jnp.where(even, pltpu.roll(a, LANE - 1, axis=1), b)
                h_glu = jnp.minimum(h_glu, SWIGLU_LIMIT)
                h_lin = jnp.clip(h_lin, -SWIGLU_LIMIT, SWIGLU_LIMIT)
                acts.append((h_glu * jax.nn.sigmoid(SWIGLU_ALPHA * h_glu) * (h_lin + 1.0)).astype(_BF16))
            act = jnp.concatenate(acts, axis=1)
            o_ref[:, m2 * wide:(m2 + 1) * wide] = jnp.dot(act, unshuffle,
                                                          preferred_element_type=_F32).astype(o_ref.dtype)


def _down_body(we_ref, wjw_ref, wr_ref, wjo_ref, live_ref, fresh_ref, x_ref, w_ref, bd_ref, o_ref, wd_scr):
    step = pl.program_id(0)
    live = live_ref[step] == 1

    @pl.when(fresh_ref[step] == 1)
    def _():
        def convert(rc, carry):
            rows = pl.ds(pl.multiple_of(rc * DOWN_CONV_ROWS, DOWN_CONV_ROWS), DOWN_CONV_ROWS)
            wd_scr[rows, :] = w_ref[rows, :].astype(_BF16)
            return carry

        lax.fori_loop(0, D_FF // DOWN_CONV_ROWS, convert, 0)

    @pl.when(jnp.logical_not(live))
    def _():
        o_ref[...] = jnp.zeros(o_ref.shape, o_ref.dtype)

    @pl.when(live)
    def _():
        o_ref[...] = jnp.dot(x_ref[...], wd_scr[...], preferred_element_type=_F32) + bd_ref[...]


def _work_w4(w, we, wjw, wr, wjo, lv, fr):
    return (0, we[w], 0, wjw[w])


def _work_b(w, we, wjw, wr, wjo, lv, fr):
    return (we[w], 0, wjw[w])


def _work_x(w, we, wjw, wr, wjo, lv, fr):
    return (wr[w], 0)


def _work_o(w, we, wjw, wr, wjo, lv, fr):
    return (wr[w], wjo[w])


def _moe_up(work, xs, w_up, b_up, bm, tn):
    p = xs.shape[0]
    n_work = work[0].shape[0]
    blk = (_nbytes((bm, D_MODEL), _BF16) + _nbytes((D_MODEL, 2 * tn), _F32) + _nbytes((bm, tn), _BF16))
    scr = _nbytes((D_MODEL, 2 * tn), _BF16)
    return pl.pallas_call(
        functools.partial(_up_body, tn=tn),
        grid_spec=pltpu.PrefetchScalarGridSpec(
            num_scalar_prefetch=len(work),
            grid=(n_work,),
            in_specs=[pl.BlockSpec((bm, D_MODEL), _work_x),
                      pl.BlockSpec((None, None, D_MODEL, 2 * tn), _work_w4),
                      pl.BlockSpec((None, 1, 2 * tn), _work_b)],
            out_specs=pl.BlockSpec((bm, tn), _work_o),
            scratch_shapes=[pltpu.VMEM((D_MODEL, 2 * tn), _BF16)],
        ),
        out_shape=jax.ShapeDtypeStruct((p, D_FF), _BF16),
        compiler_params=_params(("arbitrary",), blk, scr + 6 * _nbytes((bm, 2 * tn), _F32)),
        name="moe_up",
    )(*work, xs, w_up, b_up)


def _moe_down(work, hact, w_down, bd, bm, tn):
    p = hact.shape[0]
    n_work = work[0].shape[0]
    blk = _nbytes((bm, D_FF), _BF16) + _nbytes((D_FF, tn), _F32) + _nbytes((bm, tn), _F32)
    scr = _nbytes((D_FF, tn), _BF16)
    return pl.pallas_call(
        _down_body,
        grid_spec=pltpu.PrefetchScalarGridSpec(
            num_scalar_prefetch=len(work),
            grid=(n_work,),
            in_specs=[pl.BlockSpec((bm, D_FF), _work_x),
                      pl.BlockSpec((None, None, D_FF, tn), _work_w4),
                      pl.BlockSpec((None, 1, tn), _work_b)],
            out_specs=pl.BlockSpec((bm, tn), _work_o),
            scratch_shapes=[pltpu.VMEM((D_FF, tn), _BF16)],
        ),
        out_shape=jax.ShapeDtypeStruct((p, D_MODEL), _F32),
        compiler_params=_params(("arbitrary",), blk, scr + 4 * _nbytes((bm, tn), _F32)),
        name="moe_down",
    )(*work, hact, w_down, bd)


def _combine_body(pos_ref, yd_hbm, gate_ref, x1_ref, g_ref, b_ref, o_ref, buf, sem, *, tb):
    def issue(t, carry):
        for k_sel in range(TOP_K):
            _row_copy(yd_hbm, pos_ref[0, 0, t * TOP_K + k_sel], buf.at[k_sel], t, sem).start()
        return carry

    def drain(t, carry):
        for k_sel in range(TOP_K):
            _row_copy(yd_hbm, 0, buf.at[k_sel], t, sem).wait()
        return carry

    lax.fori_loop(0, tb, issue, 0)
    lax.fori_loop(0, tb, drain, 0)
    gate = gate_ref[...]
    ffn = buf[0] * gate[:, 0:1]
    for k_sel in range(1, TOP_K):
        ffn = ffn + buf[k_sel] * gate[:, k_sel:k_sel + 1]
    o_ref[...] = _layer_norm(DEEPNORM_ALPHA * x1_ref[...] + ffn, g_ref[...], b_ref[...])


def _combine(yd, pos, gate_pad, x1, ln_g, ln_b, *, row0, n_rows, tb):
    nb = n_rows // tb
    rb0 = row0 // tb
    pos3 = pos.reshape(-1, 1, tb * TOP_K)
    vec = pl.BlockSpec((1, D_MODEL), lambda i: (0, 0))
    blk = 2 * _nbytes((tb, D_MODEL), _F32) + _nbytes((tb, LANE), _F32)
    return pl.pallas_call(
        functools.partial(_combine_body, tb=tb),
        grid=(nb,),
        in_specs=[pl.BlockSpec((1, 1, tb * TOP_K), lambda i: (rb0 + i, 0, 0), memory_space=pltpu.SMEM),
                  pl.BlockSpec(memory_space=pl.ANY),
                  pl.BlockSpec((tb, LANE), lambda i: (rb0 + i, 0)),
                  pl.BlockSpec((tb, D_MODEL), lambda i: (rb0 + i, 0)), vec, vec],
        out_specs=pl.BlockSpec((tb, D_MODEL), lambda i: (i, 0)),
        out_shape=jax.ShapeDtypeStruct((n_rows, D_MODEL), _F32),
        scratch_shapes=[pltpu.VMEM((TOP_K, tb, D_MODEL), _F32), pltpu.SemaphoreType.DMA(())],
        compiler_params=_params(("arbitrary",), blk, (TOP_K + 4) * _nbytes((tb, D_MODEL), _F32)),
        name=f"moe_combine_ln2_r{row0}",
    )(pos3, yd, gate_pad, x1, ln_g, ln_b)


def _routing(top_idx, bm):
    n_tok = top_idx.shape[0]
    n_assign = n_tok * TOP_K
    assert n_assign % LANE == 0
    n_chunks = n_assign // LANE
    n_blocks_max = (n_assign + N_EXPERTS * (bm - 1) + bm - 1) // bm
    flat_e = top_idx.reshape(-1)
    flat_tok = jnp.repeat(jnp.arange(n_tok, dtype=jnp.int32), TOP_K)
    onehot = flat_e[:, None] == jnp.arange(N_EXPERTS, dtype=jnp.int32)[None, :]
    oh3 = onehot.reshape(n_chunks, LANE, N_EXPERTS).astype(_BF16)
    incl = jnp.tril(jnp.ones((LANE, LANE), _BF16))
    within = jnp.einsum('ij,cje->cie', incl, oh3, preferred_element_type=_F32)
    chunk_tot = within[:, -1, :]
    before = jnp.tril(jnp.ones((n_chunks, n_chunks), _BF16), k=-1)
    chunk_off = jnp.dot(before, chunk_tot.astype(_BF16), preferred_element_type=_F32)
    csum = (within + chunk_off[:, None, :]).reshape(n_assign, N_EXPERTS)
    rank = jnp.sum(jnp.where(onehot, csum, 0.0), axis=1).astype(jnp.int32) - 1
    counts = (chunk_off[-1] + chunk_tot[-1]).astype(jnp.int32)
    nblk_e = (counts + bm - 1) // bm
    blk_end = jnp.cumsum(nblk_e)
    blk_start = blk_end - nblk_e
    dest = (blk_start[flat_e] * bm + rank).astype(jnp.int32)
    slot_tok = jnp.zeros((n_blocks_max * bm,), jnp.int32).at[dest].set(flat_tok, unique_indices=True)
    n_live_blocks = blk_end[-1].astype(jnp.int32)
    return slot_tok, dest.reshape(n_tok, TOP_K), n_live_blocks, (nblk_e, blk_start, blk_end, n_blocks_max)


def _work_list(layout, n_live_blocks, n_col_tiles):
    nblk_e, blk_start, blk_end, n_blocks_max = layout
    n_work = n_blocks_max * n_col_tiles
    n_live_work = n_live_blocks * n_col_tiles
    w_all = jnp.arange(n_work, dtype=jnp.int32)
    live = w_all < n_live_work
    w = jnp.minimum(w_all, n_live_work - 1)
    work_end = blk_end * n_col_tiles
    we = jnp.minimum(jnp.sum((work_end[None, :] <= w[:, None]).astype(jnp.int32), axis=1), N_EXPERTS - 1)
    local = w - blk_start[we] * n_col_tiles
    nb_e = jnp.maximum(nblk_e[we], 1)
    wjw = (local // nb_e).astype(jnp.int32)
    dead = w_all - n_live_work
    wr = jnp.where(live, blk_start[we] + local % nb_e, n_live_blocks + dead // n_col_tiles).astype(jnp.int32)
    wjo = jnp.where(live, wjw, dead % n_col_tiles).astype(jnp.int32)
    tile_id = we * n_col_tiles + wjw
    fresh = live & jnp.concatenate([jnp.ones((1,), bool), tile_id[1:] != tile_id[:-1]])
    return (we.astype(jnp.int32), wjw, wr, wjo, live.astype(jnp.int32), fresh.astype(jnp.int32))


def kernel(x_prompt, x_sample, mem_prompt, cache_mem_k, cache_mem_v, state_gla, state_pool, w_in, w_fg, b_fg, gla_norm_g, pool_w, pool_scale, w_mem_k, w_mem_v, w_br_gla, w_br_pool, w_br_mem, b_gate, w_out, ln1_g, ln1_b, w_router, b_router, w_up, b_up, w_down, b_down, ln2_g, ln2_b):
    assert w_in.shape[0] == DEPTH == 1
    bp, tp, _ = x_prompt.shape
    bs, ts, _ = x_sample.shape
    n_p, n_s = bp * tp, bs * ts
    n_tok = n_p + n_s
    m_tok = mem_prompt.shape[1]

    x_all = jnp.concatenate([x_prompt.reshape(n_p, D_MODEL), x_sample.reshape(n_s, D_MODEL)], axis=0)
    x_bf = x_all.astype(_BF16)
    w_in0 = w_in[0]
    w_main = jnp.concatenate([w_in0[:, :FG_COL0], w_in0[:, FG_COL0 + GLA_GATE_RANK:]], axis=1).astype(_BF16)
    w_fgl = jnp.pad(w_in0[:, FG_COL0:FG_COL0 + GLA_GATE_RANK], ((0, 0), (0, FG_PAD - GLA_GATE_RANK))).astype(_BF16)
    wfg_pad = jnp.pad(w_fg[0], ((0, FG_PAD - GLA_GATE_RANK), (0, 0)))
    b_fg2 = b_fg[0].reshape(1, GLA_DK)
    gla_g = gla_norm_g[0].reshape(1, GLA_DV)

    z = _matmul(x_bf, w_main, 1024, 1024, name="in_proj")
    fgl = _matmul(x_bf, w_fgl, 1024, FG_PAD, name="in_proj_fg")

    mem_bf = mem_prompt.reshape(bp * m_tok, D_MODEL).astype(_BF16)
    mk = _matmul(mem_bf, w_mem_k[0].astype(_BF16), 512, 512, name="mem_k")
    mv = _matmul(mem_bf, w_mem_v[0].astype(_BF16), 512, 512, name="mem_v")
    mk_out = mk.reshape(1, bp, m_tok, MEM_HEADS, MEM_HEAD_DIM)
    mv_out = mv.reshape(1, bp, m_tok, MEM_HEADS, MEM_HEAD_DIM)

    s0_p = jnp.zeros((1, bp, GLA_HEADS, GLA_HEAD_DK, GLA_HEAD_DV), _F32)
    oa_p, gla_p = _gla(z, fgl, wfg_pad, b_fg2, gla_g, s0_p, row0=0, bsz=bp, t_len=tp, tb=256, c=GLA_SUB,
                       out_dtype=_BF16)
    oa_s, gla_s = _gla(z, fgl, wfg_pad, b_fg2, gla_g, state_gla, row0=n_p, bsz=bs, t_len=ts, tb=ts, c=ts,
                       out_dtype=_F32)

    scale2 = pool_scale[0].reshape(1, POOL_WIDTH)
    ob_p = _pool(z, z, pool_w[0], scale2, row0=0, bsz=bp, t_len=tp, tb=512, pos0=0, halo_from_z=True,
                 out_dtype=_BF16)
    halo_s = jnp.pad(state_pool[0], ((0, 0), (HALO - POOL_BUF, 0), (0, 0)))
    ob_s = _pool(z, halo_s, pool_w[0], scale2, row0=n_p, bsz=bs, t_len=ts, tb=ts, pos0=PAST_LEN, halo_from_z=False,
                 out_dtype=_F32)

    oc_p = _memattn(z, mk_out, mv_out, row0=0, bsz=bp, t_len=tp, tb=512, out_dtype=_BF16)
    oc_s = _memattn(z, cache_mem_k, cache_mem_v, row0=n_p, bsz=bs, t_len=ts, tb=ts, out_dtype=_F32)

    br_a = jnp.concatenate([oa_p, oa_s.astype(_BF16)], axis=0)
    br_b = jnp.concatenate([ob_p, ob_s.astype(_BF16)], axis=0)
    br_c = jnp.concatenate([oc_p, oc_s.astype(_BF16)], axis=0)

    h = _merge(br_a, br_b, br_c, w_br_gla[0].astype(_BF16), w_br_pool[0].astype(_BF16), w_br_mem[0].astype(_BF16),
               z, b_gate[0].reshape(1, N_BRANCHES * D_MODEL), 1024, 512)
    t_out = _matmul(h, w_out[0].astype(_BF16), 1024, 1024, name="out_proj")
    x1, idx_pad, gate_pad = _ln_router(x_all, t_out, ln1_g[0].reshape(1, D_MODEL), ln1_b[0].reshape(1, D_MODEL),
                                       w_router[0], b_router[0].reshape(1, N_EXPERTS), 256)
    top_idx = idx_pad[:, :TOP_K]

    slot_tok, pos, n_live_blocks, layout = _routing(top_idx, MOE_BM)
    xs = _dispatch(x1, slot_tok, n_live_blocks.reshape(1), MOE_BM)
    hact = _moe_up(_work_list(layout, n_live_blocks, D_FF // MOE_UP_TN), xs, w_up,
                   b_up[0].reshape(N_EXPERTS, 1, 2 * D_FF), MOE_BM, MOE_UP_TN)
    yd = _moe_down(_work_list(layout, n_live_blocks, D_MODEL // MOE_DOWN_TN), hact, w_down,
                   b_down[0].reshape(N_EXPERTS, 1, D_MODEL), MOE_BM, MOE_DOWN_TN)
    ln2 = (ln2_g[0].reshape(1, D_MODEL), ln2_b[0].reshape(1, D_MODEL))
    y_p = _combine(yd, pos, gate_pad, x1, *ln2, row0=0, n_rows=n_p, tb=128).reshape(bp, tp, D_MODEL)
    y_s = _combine(yd, pos, gate_pad, x1, *ln2, row0=n_p, n_rows=n_s, tb=128).reshape(bs, ts, D_MODEL)

    u_p = z[:n_p, COL_U:COL_U + POOL_WIDTH].reshape(bp, tp, POOL_WIDTH)
    u_s = z[n_p:, COL_U:COL_U + POOL_WIDTH].reshape(bs, ts, POOL_WIDTH)
    pool_p = u_p[:, tp - POOL_BUF:]
    pool_s = jnp.concatenate([state_pool[0], u_s], axis=1)[:, -POOL_BUF:]
    return (y_p, y_s, mk_out, mv_out, gla_p, pool_p[None], gla_s, pool_s[None])
```

```python
import functools
import math

import jax
import jax.numpy as jnp
from jax import lax
from jax.experimental import pallas as pl
from jax.experimental.pallas import tpu as pltpu

D_MODEL = 4096
DEPTH = 1
GLA_HEADS = 4
GLA_DV = D_MODEL // 2
GLA_DK = GLA_DV // 2
GLA_HEAD_DK = GLA_DK // GLA_HEADS
GLA_HEAD_DV = GLA_DV // GLA_HEADS
GLA_GATE_RANK = 16
GLA_GATE_TAU = 16.0
POOL_WIDTH = D_MODEL // 4
POOL_WINDOWS = (2, 4, 8, 16)
POOL_GROUP_DIM = POOL_WIDTH // len(POOL_WINDOWS)
POOL_BUF = max(POOL_WINDOWS) - 1
MEM_HEADS = 4
MEM_WIDTH = D_MODEL // 4
MEM_HEAD_DIM = MEM_WIDTH // MEM_HEADS
N_BRANCHES = 3
N_EXPERTS = 32
TOP_K = 4
D_FF = D_MODEL
SWIGLU_LIMIT = 7.0
SWIGLU_ALPHA = 1.702
NORM_EPS = 1e-5
DEEPNORM_ALPHA = (2.0 * DEPTH) ** 0.25
PAST_LEN = 16384

COL_Q = 0
COL_K = GLA_DK
COL_V = 2 * GLA_DK
COL_R = COL_V + GLA_DV
COL_U = COL_R + GLA_DV
COL_MQ = COL_U + POOL_WIDTH
COL_G = COL_MQ + MEM_WIDTH
Z_WIDTH = COL_G + N_BRANCHES * D_MODEL
FG_COL0 = COL_R + GLA_DV
FG_PAD = 128

V7X_VMEM_BYTES = 64 * 1024 * 1024
V7X_VMEM_REQUEST_CAP = 56 * 1024 * 1024
LANE = 128
LANE_BITS = 7

GLA_SUB = 16
MOE_BM = 256
MOE_UP_TN = 512
MOE_DOWN_TN = 1024
ROW_DMA_UNROLL = 8
CONV_ROWS = 256
HALO = 16

_HI = lax.Precision.HIGHEST
_F32 = jnp.float32
_BF16 = jnp.bfloat16


def _nbytes(shape, dtype):
    return math.prod(shape) * jnp.dtype(dtype).itemsize


def _params(semantics, block_bytes, extra_bytes=0):
    need = 2 * block_bytes + extra_bytes + (4 << 20)
    return pltpu.CompilerParams(dimension_semantics=semantics,
                                vmem_limit_bytes=int(min(max(need, 16 << 20), V7X_VMEM_REQUEST_CAP)))


def _mm_body(x_ref, w_ref, o_ref):
    o_ref[...] = jnp.dot(x_ref[...], w_ref[...], preferred_element_type=_F32).astype(o_ref.dtype)


def _matmul(x, w, bm, bn, out_dtype=_F32, name="matmul"):
    m, k = x.shape
    _, n = w.shape
    assert m % bm == 0 and n % bn == 0
    blk = _nbytes((bm, k), x.dtype) + _nbytes((k, bn), w.dtype) + _nbytes((bm, bn), out_dtype)
    return pl.pallas_call(
        _mm_body,
        grid=(m // bm, n // bn),
        in_specs=[pl.BlockSpec((bm, k), lambda i, j: (i, 0)),
                  pl.BlockSpec((k, bn), lambda i, j: (0, j))],
        out_specs=pl.BlockSpec((bm, bn), lambda i, j: (i, j)),
        out_shape=jax.ShapeDtypeStruct((m, n), out_dtype),
        compiler_params=_params(("parallel", "arbitrary"), blk, _nbytes((bm, bn), _F32)),
        name=name,
    )(x, w)


def _log_sigmoid(x):
    return jnp.minimum(x, 0.0) - jnp.log(1.0 + jnp.exp(-jnp.abs(x)))


def _gla_body(q_ref, k_ref, v_ref, r_ref, fg_ref, wfg_ref, bfg_ref, g_ref, s0_ref,
              o_ref, s_out_ref,
              st_scr, q_scr, cum_scr, qd_scr, kd_scr, tot_scr, o_scr, *, tb, c):
    t_blk = pl.program_id(2)
    n_sub = tb // c
    shift = int(math.log2(c))

    @pl.when(t_blk == 0)
    def _():
        st_scr[...] = s0_ref[...].T

    fg = jnp.dot(fg_ref[...], wfg_ref[...], precision=_HI, preferred_element_type=_F32) + bfg_ref[...]
    la = _log_sigmoid(fg) / GLA_GATE_TAU
    row = lax.broadcasted_iota(jnp.int32, (tb, tb), 0)
    col = lax.broadcasted_iota(jnp.int32, (tb, tb), 1)
    same = jnp.right_shift(row, shift) == jnp.right_shift(col, shift)
    incl = jnp.where(same & (col <= row), 1.0, 0.0).astype(_F32)
    cum = jnp.dot(incl, la, precision=_HI, preferred_element_type=_F32)
    tot = jnp.dot(same.astype(_F32), la, precision=_HI, preferred_element_type=_F32)
    q = q_ref[...] * (GLA_HEAD_DK ** -0.5)
    q_scr[...] = q
    cum_scr[...] = cum
    tot_scr[...] = tot
    qd_scr[...] = q * jnp.exp(cum)
    kd_scr[...] = k_ref[...] * jnp.exp(tot - cum)

    def sub_chunk(s, carry):
        r0 = pl.multiple_of(s * c, c)
        rows = pl.ds(r0, c)
        q_s = q_scr[rows, :]
        k_s = k_ref[rows, :]
        v_s = v_ref[rows, :]
        cum_s = cum_scr[rows, :]
        j_idx = lax.broadcasted_iota(jnp.int32, (c, 1), 0)
        outs = []
        for i in range(c):
            rel = cum_s[i:i + 1, :] - cum_s
            decay = jnp.exp(jnp.where(j_idx <= i, rel, -jnp.inf))
            score = jnp.sum(decay * k_s * q_s[i:i + 1, :], axis=1, keepdims=True)
            outs.append(jnp.sum(score * v_s, axis=0, keepdims=True))
        o_diag = jnp.concatenate(outs, axis=0)
        st = st_scr[...]
        o_state = lax.dot_general(qd_scr[rows, :].astype(_BF16), st.astype(_BF16),
                                  (((1,), (1,)), ((), ())), preferred_element_type=_F32)
        o_scr[rows, :] = o_diag + o_state
        upd = lax.dot_general(v_s.astype(_BF16), kd_scr[rows, :].astype(_BF16),
                              (((0,), (0,)), ((), ())), preferred_element_type=_F32)
        st_scr[...] = st * jnp.exp(tot_scr[pl.ds(r0, 1), :]) + upd
        return carry

    lax.fori_loop(0, n_sub, sub_chunk, 0)

    o = o_scr[...]
    o = o * lax.rsqrt(jnp.mean(o * o, axis=-1, keepdims=True) + NORM_EPS) * g_ref[...]
    r = r_ref[...]
    o_ref[...] = (o * (r * jax.nn.sigmoid(r))).astype(o_ref.dtype)

    @pl.when(t_blk == pl.num_programs(2) - 1)
    def _():
        s_out_ref[...] = st_scr[...].T


def _gla(z, fgl, wfg_pad, b_fg, gla_g, s0, *, row0, bsz, t_len, tb, c, out_dtype):
    nt = t_len // tb
    rb0 = row0 // tb
    dk, dv = GLA_HEAD_DK, GLA_HEAD_DV

    def rows(b, h, t):
        return rb0 + b * nt + t

    kernel = functools.partial(_gla_body, tb=tb, c=c)
    blk = (2 * _nbytes((tb, dk), _F32) + 2 * _nbytes((tb, dv), _F32) + _nbytes((tb, FG_PAD), _F32)
           + _nbytes((FG_PAD, dk), _F32) + 2 * _nbytes((dk, dv), _F32) + _nbytes((tb, dv), _F32))
    scr = _nbytes((dv, dk), _F32) + 5 * _nbytes((tb, dk), _F32) + _nbytes((tb, dv), _F32)
    return pl.pallas_call(
        kernel,
        grid=(bsz, GLA_HEADS, nt),
        in_specs=[
            pl.BlockSpec((tb, dk), lambda b, h, t: (rows(b, h, t), COL_Q // dk + h)),
            pl.BlockSpec((tb, dk), lambda b, h, t: (rows(b, h, t), COL_K // dk + h)),
            pl.BlockSpec((tb, dv), lambda b, h, t: (rows(b, h, t), COL_V // dv + h)),
            pl.BlockSpec((tb, dv), lambda b, h, t: (rows(b, h, t), COL_R // dv + h)),
            pl.BlockSpec((tb, FG_PAD), lambda b, h, t: (rows(b, h, t), 0)),
            pl.BlockSpec((FG_PAD, dk), lambda b, h, t: (0, h)),
            pl.BlockSpec((1, dk), lambda b, h, t: (0, h)),
            pl.BlockSpec((1, dv), lambda b, h, t: (0, h)),
            pl.BlockSpec((None, None, None, dk, dv), lambda b, h, t: (0, b, h, 0, 0)),
        ],
        out_specs=[
            pl.BlockSpec((tb, dv), lambda b, h, t: (b * nt + t, h)),
            pl.BlockSpec((None, None, None, dk, dv), lambda b, h, t: (0, b, h, 0, 0)),
        ],
        out_shape=[jax.ShapeDtypeStruct((bsz * t_len, GLA_DV), out_dtype),
                   jax.ShapeDtypeStruct((1, bsz, GLA_HEADS, dk, dv), _F32)],
        scratch_shapes=[pltpu.VMEM((dv, dk), _F32), pltpu.VMEM((tb, dk), _F32), pltpu.VMEM((tb, dk), _F32),
                        pltpu.VMEM((tb, dk), _F32), pltpu.VMEM((tb, dk), _F32), pltpu.VMEM((tb, dk), _F32),
                        pltpu.VMEM((tb, dv), _F32)],
        compiler_params=_params(("parallel", "parallel", "arbitrary"), blk, scr + (8 << 20)),
        name=f"gla_t{tb}",
    )(z, z, z, z, fgl, wfg_pad, b_fg, gla_g, s0)


def _pool_body(u_ref, halo_ref, pw_ref, scale_ref, o_ref, ext_scr, *, tb, pos0, first_block_has_no_history):
    t_blk = pl.program_id(1)
    halo = halo_ref[...]
    if first_block_has_no_history:
        halo = jnp.where(t_blk > 0, halo, 0.0)
    ext_scr[0:HALO, :] = halo
    ext_scr[HALO:HALO + tb, :] = u_ref[...]
    pos = pos0 + t_blk * tb + lax.broadcasted_iota(jnp.int32, (tb, 1), 0)
    gd = POOL_GROUP_DIM
    for gi, win in enumerate(POOL_WINDOWS):
        cols = slice(gi * gd, (gi + 1) * gd)
        tok = ext_scr[HALO:HALO + tb, cols]
        acc = tok
        for s in range(1, win):
            acc = acc + ext_scr[HALO - s:HALO - s + tb, cols]
        cnt = jnp.minimum(pos + 1, win).astype(_F32)
        diff = acc / cnt - tok
        mixed = jnp.dot(diff.astype(_BF16), pw_ref[gi].astype(_BF16), preferred_element_type=_F32)
        o_ref[:, cols] = (mixed * scale_ref[:, cols]).astype(o_ref.dtype)


def _pool(z, halo, pool_w, pool_scale, *, row0, bsz, t_len, tb, pos0, halo_from_z, out_dtype):
    nt = t_len // tb
    rb0 = row0 // tb
    ucol = COL_U // POOL_WIDTH
    kernel = functools.partial(_pool_body, tb=tb, pos0=pos0, first_block_has_no_history=halo_from_z)
    if halo_from_z:
        hb0 = row0 // HALO
        per = tb // HALO
        halo_spec = pl.BlockSpec((HALO, POOL_WIDTH),
                                 lambda b, t: (jnp.maximum(hb0 + (b * nt + t) * per - 1, 0), ucol))
    else:
        halo_spec = pl.BlockSpec((None, HALO, POOL_WIDTH), lambda b, t: (b, 0, 0))
    blk = (2 * _nbytes((tb, POOL_WIDTH), _F32) + _nbytes((HALO, POOL_WIDTH), _F32)
           + _nbytes(pool_w.shape, _F32))
    return pl.pallas_call(
        kernel,
        grid=(bsz, nt),
        in_specs=[
            pl.BlockSpec((tb, POOL_WIDTH), lambda b, t: (rb0 + b * nt + t, ucol)),
            halo_spec,
            pl.BlockSpec(pool_w.shape, lambda b, t: (0, 0, 0)),
            pl.BlockSpec((1, POOL_WIDTH), lambda b, t: (0, 0)),
        ],
        out_specs=pl.BlockSpec((tb, POOL_WIDTH), lambda b, t: (b * nt + t, 0)),
        out_shape=jax.ShapeDtypeStruct((bsz * t_len, POOL_WIDTH), out_dtype),
        scratch_shapes=[pltpu.VMEM((HALO + tb, POOL_WIDTH), _F32)],
        compiler_params=_params(("parallel", "arbitrary"), blk, 4 * _nbytes((tb, POOL_WIDTH), _F32)),
        name=f"pool_t{tb}",
    )(z, halo, pool_w, pool_scale)


def _memattn_body(q_ref, k_hbm, v_hbm, o_ref, kv_buf, sems):
    b = pl.program_id(0)
    t = pl.program_id(1)
    hd = MEM_HEAD_DIM
    slot = b % 2

    def head_copies(bb, s):
        return [pltpu.make_async_copy(src.at[0, bb, :, h, :], kv_buf.at[s, i_src, h], sems.at[s])
                for i_src, src in enumerate((k_hbm, v_hbm)) for h in range(MEM_HEADS)]

    @pl.when(t == 0)
    def _():
        @pl.when(b == 0)
        def _():
            for c in head_copies(0, 0):
                c.start()

        for c in head_copies(b, slot):
            c.wait()

        @pl.when(b + 1 < pl.num_programs(0))
        def _():
            for c in head_copies(b + 1, 1 - slot):
                c.start()

    for h in range(MEM_HEADS):
        cols = slice(h * hd, (h + 1) * hd)
        q = (q_ref[:, cols] * (hd ** -0.5)).astype(_BF16)
        k = kv_buf[slot, 0, h].astype(_BF16)
        s = lax.dot_general(q, k, (((1,), (1,)), ((), ())), preferred_element_type=_F32)
        p = jnp.exp(s - jnp.max(s, axis=-1, keepdims=True))
        p = p / jnp.sum(p, axis=-1, keepdims=True)
        o = jnp.dot(p.astype(_BF16), kv_buf[slot, 1, h].astype(_BF16), preferred_element_type=_F32)
        o_ref[:, cols] = o.astype(o_ref.dtype)


def _memattn(z, mem_k, mem_v, *, row0, bsz, t_len, tb, out_dtype):
    nt = t_len // tb
    rb0 = row0 // tb
    m_tok = mem_k.shape[2]
    kv_shape = (2, 2, MEM_HEADS, m_tok, MEM_HEAD_DIM)
    blk = 2 * _nbytes((tb, MEM_WIDTH), _F32)
    return pl.pallas_call(
        _memattn_body,
        grid=(bsz, nt),
        in_specs=[pl.BlockSpec((tb, MEM_WIDTH), lambda b, t: (rb0 + b * nt + t, COL_MQ // MEM_WIDTH)),
                  pl.BlockSpec(memory_space=pl.ANY), pl.BlockSpec(memory_space=pl.ANY)],
        out_specs=pl.BlockSpec((tb, MEM_WIDTH), lambda b, t: (b * nt + t, 0)),
        out_shape=jax.ShapeDtypeStruct((bsz * t_len, MEM_WIDTH), out_dtype),
        scratch_shapes=[pltpu.VMEM(kv_shape, _F32), pltpu.SemaphoreType.DMA((2,))],
        compiler_params=_params(("arbitrary", "arbitrary"), blk,
                                _nbytes(kv_shape, _F32) + 6 * _nbytes((tb, m_tok), _F32)),
        name=f"memattn_t{tb}",
    )(z, mem_k, mem_v)


def _merge_body(a_ref, b_ref, c_ref, wa_ref, wb_ref, wc_ref, g0_ref, g1_ref, g2_ref,
                bg0_ref, bg1_ref, bg2_ref, o_ref):
    def gated(x_ref, w_ref, g_ref, bg_ref):
        br = jnp.dot(x_ref[...], w_ref[...], preferred_element_type=_F32)
        return jax.nn.sigmoid(g_ref[...] + bg_ref[...]) * br

    h = gated(a_ref, wa_ref, g0_ref, bg0_ref)
    h = h + gated(b_ref, wb_ref, g1_ref, bg1_ref)
    h = h + gated(c_ref, wc_ref, g2_ref, bg2_ref)
    o_ref[...] = h.astype(o_ref.dtype)


def _merge(br_a, br_b, br_c, wa, wb, wc, z, b_gate, bm, bn):
    n = br_a.shape[0]
    nj = D_MODEL // bn
    g0 = COL_G // bn

    def gspec(i_br):
        return pl.BlockSpec((bm, bn), lambda i, j: (i, g0 + i_br * nj + j))

    def bspec(i_br):
        return pl.BlockSpec((1, bn), lambda i, j: (0, i_br * nj + j))

    def xspec(width):
        return pl.BlockSpec((bm, width), lambda i, j: (i, 0))

    def wspec(width):
        return pl.BlockSpec((width, bn), lambda i, j: (0, j))

    blk = (_nbytes((bm, D_MODEL), _BF16) + _nbytes((D_MODEL, bn), _BF16) + 3 * _nbytes((bm, bn), _F32)
           + _nbytes((bm, bn), _BF16))
    return pl.pallas_call(
        _merge_body,
        grid=(n // bm, nj),
        in_specs=[xspec(GLA_DV), xspec(POOL_WIDTH), xspec(MEM_WIDTH),
                  wspec(GLA_DV), wspec(POOL_WIDTH), wspec(MEM_WIDTH),
                  gspec(0), gspec(1), gspec(2), bspec(0), bspec(1), bspec(2)],
        out_specs=pl.BlockSpec((bm, bn), lambda i, j: (i, j)),
        out_shape=jax.ShapeDtypeStruct((n, D_MODEL), _BF16),
        compiler_params=_params(("parallel", "arbitrary"), blk, 4 * _nbytes((bm, bn), _F32)),
        name="merge",
    )(br_a, br_b, br_c, wa, wb, wc, z, z, z, b_gate, b_gate, b_gate)


def _layer_norm(v, g, b):
    mu = jnp.mean(v, axis=-1, keepdims=True)
    d = v - mu
    var = jnp.mean(d * d, axis=-1, keepdims=True)
    return d * lax.rsqrt(var + NORM_EPS) * g + b


def _ln_router_body(x_ref, t_ref, g_ref, b_ref, wr_ref, br_ref, x1_ref, idx_ref, gate_ref):
    x1 = _layer_norm(DEEPNORM_ALPHA * x_ref[...] + t_ref[...], g_ref[...], b_ref[...])
    x1_ref[...] = x1
    logits = jnp.dot(x1, wr_ref[...], precision=_HI, preferred_element_type=_F32) + br_ref[...]
    bm = logits.shape[0]
    e_idx = lax.broadcasted_iota(jnp.int32, logits.shape, 1)
    lane = lax.broadcasted_iota(jnp.int32, (bm, LANE), 1)
    vals, idxs = [], []
    for _ in range(TOP_K):
        m = jnp.max(logits, axis=-1, keepdims=True)
        first = jnp.min(jnp.where(logits == m, e_idx, N_EXPERTS), axis=-1, keepdims=True)
        vals.append(m)
        idxs.append(first)
        logits = jnp.where(e_idx == first, -jnp.inf, logits)
    exps = [jnp.exp(v - vals[0]) for v in vals]
    denom = exps[0] + exps[1] + exps[2] + exps[3]
    idx_out = jnp.zeros((bm, LANE), jnp.int32)
    gate_out = jnp.zeros((bm, LANE), _F32)
    for k_sel in range(TOP_K):
        idx_out = jnp.where(lane == k_sel, idxs[k_sel], idx_out)
        gate_out = jnp.where(lane == k_sel, exps[k_sel] / denom, gate_out)
    idx_ref[...] = idx_out
    gate_ref[...] = gate_out


def _ln_router(x, t, ln_g, ln_b, w_router, b_router, bm):
    n = x.shape[0]
    row = pl.BlockSpec((bm, D_MODEL), lambda i: (i, 0))
    vec = pl.BlockSpec((1, D_MODEL), lambda i: (0, 0))
    small = pl.BlockSpec((bm, LANE), lambda i: (i, 0))
    blk = 3 * _nbytes((bm, D_MODEL), _F32) + _nbytes((D_MODEL, LANE), _F32)
    return pl.pallas_call(
        _ln_router_body,
        grid=(n // bm,),
        in_specs=[row, row, vec, vec,
                  pl.BlockSpec((D_MODEL, N_EXPERTS), lambda i: (0, 0)),
                  pl.BlockSpec((1, N_EXPERTS), lambda i: (0, 0))],
        out_specs=[row, small, small],
        out_shape=[jax.ShapeDtypeStruct((n, D_MODEL), _F32),
                   jax.ShapeDtypeStruct((n, LANE), jnp.int32), jax.ShapeDtypeStruct((n, LANE), _F32)],
        compiler_params=_params(("parallel",), blk, 8 * _nbytes((bm, D_MODEL), _F32)),
        name="ln1_router",
    )(x, t, ln_g, ln_b, w_router, b_router)


def _row_copy(src_hbm, row, dst, r, sem):
    return pltpu.make_async_copy(src_hbm.at[pl.ds(row, 1), :], dst.at[pl.ds(r, 1), :], sem)


def _dispatch_body(nlive_ref, tok_ref, tok_next_ref, x_hbm, o_ref, buf, sems, *, gb):
    i = pl.program_id(0)
    n_live = nlive_ref[0]
    slot = i % 2

    def issue(tok, s):
        def one(r, carry):
            _row_copy(x_hbm, tok[0, 0, r], buf.at[s], r, sems.at[s]).start()
            return carry
        lax.fori_loop(0, gb, one, 0, unroll=ROW_DMA_UNROLL)

    @pl.when((i == 0) & (n_live > 0))
    def _():
        issue(tok_ref, 0)

    @pl.when(i + 1 < jnp.minimum(n_live, pl.num_programs(0)))
    def _():
        issue(tok_next_ref, 1 - slot)

    @pl.when(i < n_live)
    def _():
        def drain(r, carry):
            _row_copy(x_hbm, 0, buf.at[slot], r, sems.at[slot]).wait()
            return carry

        lax.fori_loop(0, gb, drain, 0, unroll=ROW_DMA_UNROLL)
        o_ref[...] = buf[slot].astype(o_ref.dtype)

    @pl.when(i >= n_live)
    def _():
        o_ref[...] = jnp.zeros(o_ref.shape, o_ref.dtype)


def _dispatch(x1, slot_tok, n_live_blocks, gb):
    p = slot_tok.shape[0]
    nb = p // gb
    tok3 = slot_tok.reshape(nb, 1, gb)
    blk = _nbytes((gb, D_MODEL), _BF16)
    return pl.pallas_call(
        functools.partial(_dispatch_body, gb=gb),
        grid_spec=pltpu.PrefetchScalarGridSpec(
            num_scalar_prefetch=1,
            grid=(nb,),
            in_specs=[pl.BlockSpec((1, 1, gb), lambda i, nl: (i, 0, 0), memory_space=pltpu.SMEM),
                      pl.BlockSpec((1, 1, gb), lambda i, nl: (jnp.minimum(i + 1, nb - 1), 0, 0),
                                   memory_space=pltpu.SMEM),
                      pl.BlockSpec(memory_space=pl.ANY)],
            out_specs=pl.BlockSpec((gb, D_MODEL), lambda i, nl: (i, 0)),
            scratch_shapes=[pltpu.VMEM((2, gb, D_MODEL), _F32), pltpu.SemaphoreType.DMA((2,))],
        ),
        out_shape=jax.ShapeDtypeStruct((p, D_MODEL), _BF16),
        compiler_params=_params(("arbitrary",), blk, 3 * _nbytes((gb, D_MODEL), _F32)),
        name="moe_dispatch",
    )(n_live_blocks, tok3, tok3, x1)


def _pair_order(p):
    return jnp.where((p & 1) == 0, p >> 1, (LANE // 2) + (p >> 1))


def _refresh_weight_tile(step, we_ref, wjw_ref, fresh_ref, next_e_ref, next_j_ref, has_next_ref,
                         w_hbm, stage, w_scr, sem):
    cols = stage.shape[1]

    def tile_copy(e, j):
        return pltpu.make_async_copy(w_hbm.at[0, e, :, pl.ds(pl.multiple_of(j * cols, cols), cols)], stage, sem)

    @pl.when(fresh_ref[step] == 1)
    def _():
        @pl.when(step == 0)
        def _():
            tile_copy(we_ref[0], wjw_ref[0]).start()

        tile_copy(we_ref[step], wjw_ref[step]).wait()

        def convert(rc, carry):
            rows = pl.ds(pl.multiple_of(rc * CONV_ROWS, CONV_ROWS), CONV_ROWS)
            w_scr[rows, :] = stage[rows, :].astype(_BF16)
            return carry

        lax.fori_loop(0, stage.shape[0] // CONV_ROWS, convert, 0)

        @pl.when(has_next_ref[step] == 1)
        def _():
            tile_copy(next_e_ref[step], next_j_ref[step]).start()


def _up_body(we_ref, wjw_ref, wr_ref, wjo_ref, live_ref, fresh_ref, next_e_ref, next_j_ref, has_next_ref,
             x_ref, w_hbm, b_ref, o_ref, stage, w_scr, sem, *, tn):
    step = pl.program_id(0)
    live = live_ref[step] == 1
    wide = 2 * LANE
    _refresh_weight_tile(step, we_ref, wjw_ref, fresh_ref, next_e_ref, next_j_ref, has_next_ref,
                         w_hbm, stage, w_scr, sem)

    @pl.when(jnp.logical_not(live))
    def _():
        o_ref[...] = jnp.zeros(o_ref.shape, o_ref.dtype)

    @pl.when(live)
    def _():
        h = jnp.dot(x_ref[...], w_scr[...], preferred_element_type=_F32) + b_ref[...]
        even = (lax.broadcasted_iota(jnp.int32, (h.shape[0], LANE), 1) & 1) == 0
        p_in = lax.broadcasted_iota(jnp.int32, (wide, wide), 0)
        p_out = lax.broadcasted_iota(jnp.int32, (wide, wide), 1)
        same_grp = (p_in >> LANE_BITS) == (p_out >> LANE_BITS)
        unshuffle = jnp.where(same_grp & (_pair_order(p_in & (LANE - 1)) == (p_out & (LANE - 1))), 1.0, 0.0)
        unshuffle = unshuffle.astype(_BF16)
        for m2 in range(tn // wide):
            acts = []
            for m in (2 * m2, 2 * m2 + 1):
                a = h[:, (2 * m) * LANE:(2 * m + 1) * LANE]
                b = h[:, (2 * m + 1) * LANE:(2 * m + 2) * LANE]
                h_glu = jnp.where(even, a, pltpu.roll(b, 1, axis=1))
                h_lin = jnp.where(even, pltpu.roll(a, LANE - 1, axis=1), b)
                h_glu = jnp.minimum(h_glu, SWIGLU_LIMIT)
                h_lin = jnp.clip(h_lin, -SWIGLU_LIMIT, SWIGLU_LIMIT)
                acts.append((h_glu * jax.nn.sigmoid(SWIGLU_ALPHA * h_glu) * (h_lin + 1.0)).astype(_BF16))
            act = jnp.concatenate(acts, axis=1)
            o_ref[:, m2 * wide:(m2 + 1) * wide] = jnp.dot(act, unshuffle,
                                                          preferred_element_type=_F32).astype(o_ref.dtype)


def _down_body(we_ref, wjw_ref, wr_ref, wjo_ref, live_ref, fresh_ref, next_e_ref, next_j_ref, has_next_ref,
               x_ref, w_hbm, bd_ref, o_ref, stage, wd_scr, sem):
    step = pl.program_id(0)
    live = live_ref[step] == 1
    _refresh_weight_tile(step, we_ref, wjw_ref, fresh_ref, next_e_ref, next_j_ref, has_next_ref,
                         w_hbm, stage, wd_scr, sem)

    @pl.when(jnp.logical_not(live))
    def _():
        o_ref[...] = jnp.zeros(o_ref.shape, o_ref.dtype)

    @pl.when(live)
    def _():
        o_ref[...] = jnp.dot(x_ref[...], wd_scr[...], preferred_element_type=_F32) + bd_ref[...]


def _work_b(w, we, wjw, wr, wjo, *_):
    return (we[w], 0, wjw[w])


def _work_x(w, we, wjw, wr, wjo, *_):
    return (wr[w], 0)


def _work_o(w, we, wjw, wr, wjo, *_):
    return (wr[w], wjo[w])


def _moe_up(work, xs, w_up, b_up, bm, tn):
    p = xs.shape[0]
    n_work = work[0].shape[0]
    blk = _nbytes((bm, D_MODEL), _BF16) + _nbytes((bm, tn), _BF16)
    scr = _nbytes((D_MODEL, 2 * tn), _F32) + _nbytes((D_MODEL, 2 * tn), _BF16)
    return pl.pallas_call(
        functools.partial(_up_body, tn=tn),
        grid_spec=pltpu.PrefetchScalarGridSpec(
            num_scalar_prefetch=len(work),
            grid=(n_work,),
            in_specs=[pl.BlockSpec((bm, D_MODEL), _work_x),
                      pl.BlockSpec(memory_space=pl.ANY),
                      pl.BlockSpec((None, 1, 2 * tn), _work_b)],
            out_specs=pl.BlockSpec((bm, tn), _work_o),
            scratch_shapes=[pltpu.VMEM((D_MODEL, 2 * tn), _F32), pltpu.VMEM((D_MODEL, 2 * tn), _BF16),
                            pltpu.SemaphoreType.DMA(())],
        ),
        out_shape=jax.ShapeDtypeStruct((p, D_FF), _BF16),
        compiler_params=_params(("arbitrary",), blk, scr + 6 * _nbytes((bm, 2 * tn), _F32)),
        name="moe_up",
    )(*work, xs, w_up, b_up)


def _moe_down(work, hact, w_down, bd, bm, tn):
    p = hact.shape[0]
    n_work = work[0].shape[0]
    blk = _nbytes((bm, D_FF), _BF16) + _nbytes((bm, tn), _F32)
    scr = _nbytes((D_FF, tn), _F32) + _nbytes((D_FF, tn), _BF16)
    return pl.pallas_call(
        _down_body,
        grid_spec=pltpu.PrefetchScalarGridSpec(
            num_scalar_prefetch=len(work),
            grid=(n_work,),
            in_specs=[pl.BlockSpec((bm, D_FF), _work_x),
                      pl.BlockSpec(memory_space=pl.ANY),
                      pl.BlockSpec((None, 1, tn), _work_b)],
            out_specs=pl.BlockSpec((bm, tn), _work_o),
            scratch_shapes=[pltpu.VMEM((D_FF, tn), _F32), pltpu.VMEM((D_FF, tn), _BF16),
                            pltpu.SemaphoreType.DMA(())],
        ),
        out_shape=jax.ShapeDtypeStruct((p, D_MODEL), _F32),
        compiler_params=_params(("arbitrary",), blk, scr + 4 * _nbytes((bm, tn), _F32)),
        name="moe_down",
    )(*work, hact, w_down, bd)


def _combine_body(pos_ref, yd_hbm, gate_ref, x1_ref, g_ref, b_ref, o_ref, buf, sem, *, tb):
    def issue(t, carry):
        for k_sel in range(TOP_K):
            _row_copy(yd_hbm, pos_ref[0, 0, t * TOP_K + k_sel], buf.at[k_sel], t, sem).start()
        return carry

    def drain(t, carry):
        for k_sel in range(TOP_K):
            _row_copy(yd_hbm, 0, buf.at[k_sel], t, sem).wait()
        return carry

    lax.fori_loop(0, tb, issue, 0)
    lax.fori_loop(0, tb, drain, 0)
    gate = gate_ref[...]
    ffn = buf[0] * gate[:, 0:1]
    for k_sel in range(1, TOP_K):
        ffn = ffn + buf[k_sel] * gate[:, k_sel:k_sel + 1]
    o_ref[...] = _layer_norm(DEEPNORM_ALPHA * x1_ref[...] + ffn, g_ref[...], b_ref[...])


def _combine(yd, pos, gate_pad, x1, ln_g, ln_b, *, row0, n_rows, tb):
    nb = n_rows // tb
    rb0 = row0 // tb
    pos3 = pos.reshape(-1, 1, tb * TOP_K)
    vec = pl.BlockSpec((1, D_MODEL), lambda i: (0, 0))
    blk = 2 * _nbytes((tb, D_MODEL), _F32) + _nbytes((tb, LANE), _F32)
    return pl.pallas_call(
        functools.partial(_combine_body, tb=tb),
        grid=(nb,),
        in_specs=[pl.BlockSpec((1, 1, tb * TOP_K), lambda i: (rb0 + i, 0, 0), memory_space=pltpu.SMEM),
                  pl.BlockSpec(memory_space=pl.ANY),
                  pl.BlockSpec((tb, LANE), lambda i: (rb0 + i, 0)),
                  pl.BlockSpec((tb, D_MODEL), lambda i: (rb0 + i, 0)), vec, vec],
        out_specs=pl.BlockSpec((tb, D_MODEL), lambda i: (i, 0)),
        out_shape=jax.ShapeDtypeStruct((n_rows, D_MODEL), _F32),
        scratch_shapes=[pltpu.VMEM((TOP_K, tb, D_MODEL), _F32), pltpu.SemaphoreType.DMA(())],
        compiler_params=_params(("arbitrary",), blk, (TOP_K + 4) * _nbytes((tb, D_MODEL), _F32)),
        name=f"moe_combine_ln2_r{row0}",
    )(pos3, yd, gate_pad, x1, ln_g, ln_b)


def _routing(top_idx, bm):
    n_tok = top_idx.shape[0]
    n_assign = n_tok * TOP_K
    assert n_assign % LANE == 0
    n_chunks = n_assign // LANE
    n_blocks_max = (n_assign + N_EXPERTS * (bm - 1) + bm - 1) // bm
    flat_e = top_idx.reshape(-1)
    flat_tok = jnp.repeat(jnp.arange(n_tok, dtype=jnp.int32), TOP_K)
    onehot = flat_e[:, None] == jnp.arange(N_EXPERTS, dtype=jnp.int32)[None, :]
    oh3 = onehot.reshape(n_chunks, LANE, N_EXPERTS).astype(_BF16)
    incl = jnp.tril(jnp.ones((LANE, LANE), _BF16))
    within = jnp.einsum('ij,cje->cie', incl, oh3, preferred_element_type=_F32)
    chunk_tot = within[:, -1, :]
    before = jnp.tril(jnp.ones((n_chunks, n_chunks), _BF16), k=-1)
    chunk_off = jnp.dot(before, chunk_tot.astype(_BF16), preferred_element_type=_F32)
    csum = (within + chunk_off[:, None, :]).reshape(n_assign, N_EXPERTS)
    rank = jnp.sum(jnp.where(onehot, csum, 0.0), axis=1).astype(jnp.int32) - 1
    counts = (chunk_off[-1] + chunk_tot[-1]).astype(jnp.int32)
    nblk_e = (counts + bm - 1) // bm
    blk_end = jnp.cumsum(nblk_e)
    blk_start = blk_end - nblk_e
    dest = (blk_start[flat_e] * bm + rank).astype(jnp.int32)
    slot_tok = jnp.zeros((n_blocks_max * bm,), jnp.int32).at[dest].set(flat_tok, unique_indices=True)
    n_live_blocks = blk_end[-1].astype(jnp.int32)
    return slot_tok, dest.reshape(n_tok, TOP_K), n_live_blocks, (nblk_e, blk_start, blk_end, n_blocks_max)


def _work_list(layout, n_live_blocks, n_col_tiles):
    nblk_e, blk_start, blk_end, n_blocks_max = layout
    n_work = n_blocks_max * n_col_tiles
    n_live_work = n_live_blocks * n_col_tiles
    w_all = jnp.arange(n_work, dtype=jnp.int32)
    live = w_all < n_live_work
    w = jnp.minimum(w_all, n_live_work - 1)
    work_end = blk_end * n_col_tiles
    we = jnp.minimum(jnp.sum((work_end[None, :] <= w[:, None]).astype(jnp.int32), axis=1), N_EXPERTS - 1)
    local = w - blk_start[we] * n_col_tiles
    nb_e = jnp.maximum(nblk_e[we], 1)
    wjw = (local // nb_e).astype(jnp.int32)
    dead = w_all - n_live_work
    wr = jnp.where(live, blk_start[we] + local % nb_e, n_live_blocks + dead // n_col_tiles).astype(jnp.int32)
    wjo = jnp.where(live, wjw, dead % n_col_tiles).astype(jnp.int32)
    tile_id = we * n_col_tiles + wjw
    fresh = live & jnp.concatenate([jnp.ones((1,), bool), tile_id[1:] != tile_id[:-1]])
    next_w = w_all + nblk_e[we]
    has_next = fresh & (next_w < n_live_work)
    next_w = jnp.minimum(next_w, n_work - 1)
    as_i32 = lambda a: a.astype(jnp.int32)
    return (as_i32(we), wjw, wr, wjo, as_i32(live), as_i32(fresh), as_i32(we[next_w]), wjw[next_w], as_i32(has_next))


def kernel(x_prompt, x_sample, mem_prompt, cache_mem_k, cache_mem_v, state_gla, state_pool, w_in, w_fg, b_fg, gla_norm_g, pool_w, pool_scale, w_mem_k, w_mem_v, w_br_gla, w_br_pool, w_br_mem, b_gate, w_out, ln1_g, ln1_b, w_router, b_router, w_up, b_up, w_down, b_down, ln2_g, ln2_b):
    assert w_in.shape[0] == DEPTH == 1
    bp, tp, _ = x_prompt.shape
    bs, ts, _ = x_sample.shape
    n_p, n_s = bp * tp, bs * ts
    n_tok = n_p + n_s
    m_tok = mem_prompt.shape[1]

    x_all = jnp.concatenate([x_prompt.reshape(n_p, D_MODEL), x_sample.reshape(n_s, D_MODEL)], axis=0)
    x_bf = x_all.astype(_BF16)
    w_in0 = w_in[0]
    w_main = jnp.concatenate([w_in0[:, :FG_COL0], w_in0[:, FG_COL0 + GLA_GATE_RANK:]], axis=1).astype(_BF16)
    w_fgl = jnp.pad(w_in0[:, FG_COL0:FG_COL0 + GLA_GATE_RANK], ((0, 0), (0, FG_PAD - GLA_GATE_RANK))).astype(_BF16)
    wfg_pad = jnp.pad(w_fg[0], ((0, FG_PAD - GLA_GATE_RANK), (0, 0)))
    b_fg2 = b_fg[0].reshape(1, GLA_DK)
    gla_g = gla_norm_g[0].reshape(1, GLA_DV)

    z = _matmul(x_bf, w_main, 1024, 1024, name="in_proj")
    fgl = _matmul(x_bf, w_fgl, 1024, FG_PAD, name="in_proj_fg")

    mem_bf = mem_prompt.reshape(bp * m_tok, D_MODEL).astype(_BF16)
    mk = _matmul(mem_bf, w_mem_k[0].astype(_BF16), 512, 512, name="mem_k")
    mv = _matmul(mem_bf, w_mem_v[0].astype(_BF16), 512, 512, name="mem_v")
    mk_out = mk.reshape(1, bp, m_tok, MEM_HEADS, MEM_HEAD_DIM)
    mv_out = mv.reshape(1, bp, m_tok, MEM_HEADS, MEM_HEAD_DIM)

    s0_p = jnp.zeros((1, bp, GLA_HEADS, GLA_HEAD_DK, GLA_HEAD_DV), _F32)
    oa_p, gla_p = _gla(z, fgl, wfg_pad, b_fg2, gla_g, s0_p, row0=0, bsz=bp, t_len=tp, tb=256, c=GLA_SUB,
                       out_dtype=_BF16)
    oa_s, gla_s = _gla(z, fgl, wfg_pad, b_fg2, gla_g, state_gla, row0=n_p, bsz=bs, t_len=ts, tb=ts, c=ts,
                       out_dtype=_F32)

    scale2 = pool_scale[0].reshape(1, POOL_WIDTH)
    ob_p = _pool(z, z, pool_w[0], scale2, row0=0, bsz=bp, t_len=tp, tb=512, pos0=0, halo_from_z=True,
                 out_dtype=_BF16)
    halo_s = jnp.pad(state_pool[0], ((0, 0), (HALO - POOL_BUF, 0), (0, 0)))
    ob_s = _pool(z, halo_s, pool_w[0], scale2, row0=n_p, bsz=bs, t_len=ts, tb=ts, pos0=PAST_LEN, halo_from_z=False,
                 out_dtype=_F32)

    oc_p = _memattn(z, mk_out, mv_out, row0=0, bsz=bp, t_len=tp, tb=512, out_dtype=_BF16)
    oc_s = _memattn(z, cache_mem_k, cache_mem_v, row0=n_p, bsz=bs, t_len=ts, tb=ts, out_dtype=_F32)

    br_a = jnp.concatenate([oa_p, oa_s.astype(_BF16)], axis=0)
    br_b = jnp.concatenate([ob_p, ob_s.astype(_BF16)], axis=0)
    br_c = jnp.concatenate([oc_p, oc_s.astype(_BF16)], axis=0)

    h = _merge(br_a, br_b, br_c, w_br_gla[0].astype(_BF16), w_br_pool[0].astype(_BF16), w_br_mem[0].astype(_BF16),
               z, b_gate[0].reshape(1, N_BRANCHES * D_MODEL), 1024, 512)
    t_out = _matmul(h, w_out[0].astype(_BF16), 1024, 1024, name="out_proj")
    x1, idx_pad, gate_pad = _ln_router(x_all, t_out, ln1_g[0].reshape(1, D_MODEL), ln1_b[0].reshape(1, D_MODEL),
                                       w_router[0], b_router[0].reshape(1, N_EXPERTS), 256)
    top_idx = idx_pad[:, :TOP_K]

    slot_tok, pos, n_live_blocks, layout = _routing(top_idx, MOE_BM)
    xs = _dispatch(x1, slot_tok, n_live_blocks.reshape(1), MOE_BM)
    hact = _moe_up(_work_list(layout, n_live_blocks, D_FF // MOE_UP_TN), xs, w_up,
                   b_up[0].reshape(N_EXPERTS, 1, 2 * D_FF), MOE_BM, MOE_UP_TN)
    yd = _moe_down(_work_list(layout, n_live_blocks, D_MODEL // MOE_DOWN_TN), hact, w_down,
                   b_down[0].reshape(N_EXPERTS, 1, D_MODEL), MOE_BM, MOE_DOWN_TN)
    ln2 = (ln2_g[0].reshape(1, D_MODEL), ln2_b[0].reshape(1, D_MODEL))
    y_p = _combine(yd, pos, gate_pad, x1, *ln2, row0=0, n_rows=n_p, tb=128).reshape(bp, tp, D_MODEL)
    y_s = _combine(yd, pos, gate_pad, x1, *ln2, row0=n_p, n_rows=n_s, tb=128).reshape(bs, ts, D_MODEL)

    u_p = z[:n_p, COL_U:COL_U + POOL_WIDTH].reshape(bp, tp, POOL_WIDTH)
    u_s = z[n_p:, COL_U:COL_U + POOL_WIDTH].reshape(bs, ts, POOL_WIDTH)
    pool_p = u_p[:, tp - POOL_BUF:]
    pool_s = jnp.concatenate([state_pool[0], u_s], axis=1)[:, -POOL_BUF:]
    return (y_p, y_s, mk_out, mv_out, gla_p, pool_p[None], gla_s, pool_s[None])
```

```python
import functools
import math

import jax
import jax.numpy as jnp
from jax import lax
from jax.experimental import pallas as pl
from jax.experimental.pallas import tpu as pltpu

D_MODEL = 4096
DEPTH = 1
GLA_HEADS = 4
GLA_DV = D_MODEL // 2
GLA_DK = GLA_DV // 2
GLA_HEAD_DK = GLA_DK // GLA_HEADS
GLA_HEAD_DV = GLA_DV // GLA_HEADS
GLA_GATE_RANK = 16
GLA_GATE_TAU = 16.0
POOL_WIDTH = D_MODEL // 4
POOL_WINDOWS = (2, 4, 8, 16)
POOL_GROUP_DIM = POOL_WIDTH // len(POOL_WINDOWS)
POOL_BUF = max(POOL_WINDOWS) - 1
MEM_HEADS = 4
MEM_WIDTH = D_MODEL // 4
MEM_HEAD_DIM = MEM_WIDTH // MEM_HEADS
N_BRANCHES = 3
N_EXPERTS = 32
TOP_K = 4
D_FF = D_MODEL
SWIGLU_LIMIT = 7.0
SWIGLU_ALPHA = 1.702
NORM_EPS = 1e-5
DEEPNORM_ALPHA = (2.0 * DEPTH) ** 0.25
PAST_LEN = 16384

COL_Q = 0
COL_K = GLA_DK
COL_V = 2 * GLA_DK
COL_R = COL_V + GLA_DV
ZA_WIDTH = COL_R + GLA_DV
FG_COL0 = ZA_WIDTH
ZB_COL0 = FG_COL0 + GLA_GATE_RANK
COL_U = 0
COL_MQ = COL_U + POOL_WIDTH
COL_G = COL_MQ + MEM_WIDTH
ZB_WIDTH = COL_G + N_BRANCHES * D_MODEL
FG_PAD = 128

V7X_VMEM_BYTES = 64 * 1024 * 1024
V7X_VMEM_REQUEST_CAP = 56 * 1024 * 1024
LANE = 128
LANE_BITS = 7

GLA_SUB = 16
MOE_BM = 256
MOE_UP_TN = 512
MOE_DOWN_TN = 1024
ROW_DMA_UNROLL = 8
CONV_ROWS = 256
HALO = 16

_HI = lax.Precision.HIGHEST
_F32 = jnp.float32
_BF16 = jnp.bfloat16


def _nbytes(shape, dtype):
    return math.prod(shape) * jnp.dtype(dtype).itemsize


def _params(semantics, block_bytes, extra_bytes=0):
    need = 2 * block_bytes + extra_bytes + (4 << 20)
    return pltpu.CompilerParams(dimension_semantics=semantics,
                                vmem_limit_bytes=int(min(max(need, 16 << 20), V7X_VMEM_REQUEST_CAP)))


def _mm_body(x_ref, w_ref, o_ref):
    o_ref[...] = jnp.dot(x_ref[...], w_ref[...], preferred_element_type=_F32).astype(o_ref.dtype)


def _matmul(x, w, bm, bn, out_dtype=_F32, name="matmul"):
    m, k = x.shape
    _, n = w.shape
    assert m % bm == 0 and n % bn == 0
    blk = _nbytes((bm, k), x.dtype) + _nbytes((k, bn), w.dtype) + _nbytes((bm, bn), out_dtype)
    return pl.pallas_call(
        _mm_body,
        grid=(m // bm, n // bn),
        in_specs=[pl.BlockSpec((bm, k), lambda i, j: (i, 0)),
                  pl.BlockSpec((k, bn), lambda i, j: (0, j))],
        out_specs=pl.BlockSpec((bm, bn), lambda i, j: (i, j)),
        out_shape=jax.ShapeDtypeStruct((m, n), out_dtype),
        compiler_params=_params(("parallel", "arbitrary"), blk, _nbytes((bm, bn), _F32)),
        name=name,
    )(x, w)


def _proj_body(x_ref, wa_ref, *rest, lane_shift):
    if lane_shift:
        wb_ref, o_ref, w_scr = rest
    else:
        o_ref, w_scr = rest
    bn = w_scr.shape[1]

    @pl.when(pl.program_id(1) == 0)
    def _():
        keep = lax.broadcasted_iota(jnp.int32, (CONV_ROWS, LANE), 1) < LANE - lane_shift

        def convert(rc, carry):
            rows = pl.ds(pl.multiple_of(rc * CONV_ROWS, CONV_ROWS), CONV_ROWS)
            if not lane_shift:
                w_scr[rows, :] = wa_ref[rows, :].astype(_BF16)
                return carry
            n_lt = bn // LANE
            rolled = [pltpu.roll(wa_ref[rows, c * LANE:(c + 1) * LANE], LANE - lane_shift, axis=1)
                      for c in range(n_lt)]
            rolled.append(pltpu.roll(wb_ref[rows, :], LANE - lane_shift, axis=1))
            for c in range(n_lt):
                w_scr[rows, c * LANE:(c + 1) * LANE] = jnp.where(keep, rolled[c], rolled[c + 1]).astype(_BF16)
            return carry

        lax.fori_loop(0, w_scr.shape[0] // CONV_ROWS, convert, 0)

    o_ref[...] = jnp.dot(x_ref[...], w_scr[...], preferred_element_type=_F32)


def _proj(x, w_in, *, col0, width, bm, bn, name):
    m, k = x.shape
    lane_shift = col0 % LANE
    base = col0 - lane_shift
    assert m % bm == 0 and width % bn == 0 and base % bn == 0
    in_specs = [pl.BlockSpec((bm, k), lambda j, i: (i, 0)),
                pl.BlockSpec((None, k, bn), lambda j, i: (0, 0, base // bn + j))]
    operands = [x, w_in]
    blk = _nbytes((bm, k), x.dtype) + _nbytes((k, bn), _F32) + _nbytes((bm, bn), _F32)
    if lane_shift:
        in_specs.append(pl.BlockSpec((None, k, LANE), lambda j, i: (0, 0, (base + (j + 1) * bn) // LANE)))
        operands.append(w_in)
        blk += _nbytes((k, LANE), _F32)
    return pl.pallas_call(
        functools.partial(_proj_body, lane_shift=lane_shift),
        grid=(width // bn, m // bm),
        in_specs=in_specs,
        out_specs=pl.BlockSpec((bm, bn), lambda j, i: (i, j)),
        out_shape=jax.ShapeDtypeStruct((m, width), _F32),
        scratch_shapes=[pltpu.VMEM((k, bn), _BF16)],
        compiler_params=_params(("arbitrary", "arbitrary"), blk, _nbytes((k, bn), _BF16) + _nbytes((bm, bn), _F32)),
        name=name,
    )(*operands)


def _log_sigmoid(x):
    return jnp.minimum(x, 0.0) - jnp.log(1.0 + jnp.exp(-jnp.abs(x)))


def _gla_body(q_ref, k_ref, v_ref, r_ref, fg_ref, wfg_ref, bfg_ref, g_ref, s0_ref,
              o_ref, s_out_ref,
              st_scr, q_scr, cum_scr, qd_scr, kd_scr, tot_scr, o_scr, *, tb, c):
    t_blk = pl.program_id(2)
    n_sub = tb // c
    shift = int(math.log2(c))

    @pl.when(t_blk == 0)
    def _():
        st_scr[...] = s0_ref[...].T

    fg = jnp.dot(fg_ref[...], wfg_ref[...], precision=_HI, preferred_element_type=_F32) + bfg_ref[...]
    la = _log_sigmoid(fg) / GLA_GATE_TAU
    row = lax.broadcasted_iota(jnp.int32, (tb, tb), 0)
    col = lax.broadcasted_iota(jnp.int32, (tb, tb), 1)
    same = jnp.right_shift(row, shift) == jnp.right_shift(col, shift)
    incl = jnp.where(same & (col <= row), 1.0, 0.0).astype(_F32)
    cum = jnp.dot(incl, la, precision=_HI, preferred_element_type=_F32)
    tot = jnp.dot(same.astype(_F32), la, precision=_HI, preferred_element_type=_F32)
    q = q_ref[...] * (GLA_HEAD_DK ** -0.5)
    q_scr[...] = q
    cum_scr[...] = cum
    tot_scr[...] = tot
    qd_scr[...] = q * jnp.exp(cum)
    kd_scr[...] = k_ref[...] * jnp.exp(tot - cum)

    def sub_chunk(s, carry):
        r0 = pl.multiple_of(s * c, c)
        rows = pl.ds(r0, c)
        q_s = q_scr[rows, :]
        k_s = k_ref[rows, :]
        v_s = v_ref[rows, :]
        cum_s = cum_scr[rows, :]
        j_idx = lax.broadcasted_iota(jnp.int32, (c, 1), 0)
        outs = []
        for i in range(c):
            rel = cum_s[i:i + 1, :] - cum_s
            decay = jnp.exp(jnp.where(j_idx <= i, rel, -jnp.inf))
            score = jnp.sum(decay * k_s * q_s[i:i + 1, :], axis=1, keepdims=True)
            outs.append(jnp.sum(score * v_s, axis=0, keepdims=True))
        o_diag = jnp.concatenate(outs, axis=0)
        st = st_scr[...]
        o_state = lax.dot_general(qd_scr[rows, :].astype(_BF16), st.astype(_BF16),
                                  (((1,), (1,)), ((), ())), preferred_element_type=_F32)
        o_scr[rows, :] = o_diag + o_state
        upd = lax.dot_general(v_s.astype(_BF16), kd_scr[rows, :].astype(_BF16),
                              (((0,), (0,)), ((), ())), preferred_element_type=_F32)
        st_scr[...] = st * jnp.exp(tot_scr[pl.ds(r0, 1), :]) + upd
        return carry

    lax.fori_loop(0, n_sub, sub_chunk, 0)

    o = o_scr[...]
    o = o * lax.rsqrt(jnp.mean(o * o, axis=-1, keepdims=True) + NORM_EPS) * g_ref[...]
    r = r_ref[...]
    o_ref[...] = (o * (r * jax.nn.sigmoid(r))).astype(o_ref.dtype)

    @pl.when(t_blk == pl.num_programs(2) - 1)
    def _():
        s_out_ref[...] = st_scr[...].T


def _gla(z, fgl, wfg_pad, b_fg, gla_g, s0, *, row0, bsz, t_len, tb, c, out_dtype):
    nt = t_len // tb
    rb0 = row0 // tb
    dk, dv = GLA_HEAD_DK, GLA_HEAD_DV

    def rows(b, h, t):
        return rb0 + b * nt + t

    kernel = functools.partial(_gla_body, tb=tb, c=c)
    blk = (2 * _nbytes((tb, dk), _F32) + 2 * _nbytes((tb, dv), _F32) + _nbytes((tb, FG_PAD), _F32)
           + _nbytes((FG_PAD, dk), _F32) + 2 * _nbytes((dk, dv), _F32) + _nbytes((tb, dv), _F32))
    scr = _nbytes((dv, dk), _F32) + 5 * _nbytes((tb, dk), _F32) + _nbytes((tb, dv), _F32)
    return pl.pallas_call(
        kernel,
        grid=(bsz, GLA_HEADS, nt),
        in_specs=[
            pl.BlockSpec((tb, dk), lambda b, h, t: (rows(b, h, t), COL_Q // dk + h)),
            pl.BlockSpec((tb, dk), lambda b, h, t: (rows(b, h, t), COL_K // dk + h)),
            pl.BlockSpec((tb, dv), lambda b, h, t: (rows(b, h, t), COL_V // dv + h)),
            pl.BlockSpec((tb, dv), lambda b, h, t: (rows(b, h, t), COL_R // dv + h)),
            pl.BlockSpec((tb, FG_PAD), lambda b, h, t: (rows(b, h, t), 0)),
            pl.BlockSpec((FG_PAD, dk), lambda b, h, t: (0, h)),
            pl.BlockSpec((1, dk), lambda b, h, t: (0, h)),
            pl.BlockSpec((1, dv), lambda b, h, t: (0, h)),
            pl.BlockSpec((None, None, None, dk, dv), lambda b, h, t: (0, b, h, 0, 0)),
        ],
        out_specs=[
            pl.BlockSpec((tb, dv), lambda b, h, t: (b * nt + t, h)),
            pl.BlockSpec((None, None, None, dk, dv), lambda b, h, t: (0, b, h, 0, 0)),
        ],
        out_shape=[jax.ShapeDtypeStruct((bsz * t_len, GLA_DV), out_dtype),
                   jax.ShapeDtypeStruct((1, bsz, GLA_HEADS, dk, dv), _F32)],
        scratch_shapes=[pltpu.VMEM((dv, dk), _F32), pltpu.VMEM((tb, dk), _F32), pltpu.VMEM((tb, dk), _F32),
                        pltpu.VMEM((tb, dk), _F32), pltpu.VMEM((tb, dk), _F32), pltpu.VMEM((tb, dk), _F32),
                        pltpu.VMEM((tb, dv), _F32)],
        compiler_params=_params(("parallel", "parallel", "arbitrary"), blk, scr + (8 << 20)),
        name=f"gla_t{tb}",
    )(z, z, z, z, fgl, wfg_pad, b_fg, gla_g, s0)


def _pool_body(u_ref, halo_ref, pw_ref, scale_ref, o_ref, ext_scr, *, tb, pos0, first_block_has_no_history):
    t_blk = pl.program_id(1)
    halo = halo_ref[...]
    if first_block_has_no_history:
        halo = jnp.where(t_blk > 0, halo, 0.0)
    ext_scr[0:HALO, :] = halo
    ext_scr[HALO:HALO + tb, :] = u_ref[...]
    pos = pos0 + t_blk * tb + lax.broadcasted_iota(jnp.int32, (tb, 1), 0)
    gd = POOL_GROUP_DIM
    for gi, win in enumerate(POOL_WINDOWS):
        cols = slice(gi * gd, (gi + 1) * gd)
        tok = ext_scr[HALO:HALO + tb, cols]
        acc = tok
        for s in range(1, win):
            acc = acc + ext_scr[HALO - s:HALO - s + tb, cols]
        cnt = jnp.minimum(pos + 1, win).astype(_F32)
        diff = acc / cnt - tok
        mixed = jnp.dot(diff.astype(_BF16), pw_ref[gi].astype(_BF16), preferred_element_type=_F32)
        o_ref[:, cols] = (mixed * scale_ref[:, cols]).astype(o_ref.dtype)


def _pool(z, halo, pool_w, pool_scale, *, row0, bsz, t_len, tb, pos0, halo_from_z, out_dtype):
    nt = t_len // tb
    rb0 = row0 // tb
    ucol = COL_U // POOL_WIDTH
    kernel = functools.partial(_pool_body, tb=tb, pos0=pos0, first_block_has_no_history=halo_from_z)
    if halo_from_z:
        hb0 = row0 // HALO
        per = tb // HALO
        halo_spec = pl.BlockSpec((HALO, POOL_WIDTH),
                                 lambda b, t: (jnp.maximum(hb0 + (b * nt + t) * per - 1, 0), ucol))
    else:
        halo_spec = pl.BlockSpec((None, HALO, POOL_WIDTH), lambda b, t: (b, 0, 0))
    blk = (2 * _nbytes((tb, POOL_WIDTH), _F32) + _nbytes((HALO, POOL_WIDTH), _F32)
           + _nbytes(pool_w.shape, _F32))
    return pl.pallas_call(
        kernel,
        grid=(bsz, nt),
        in_specs=[
            pl.BlockSpec((tb, POOL_WIDTH), lambda b, t: (rb0 + b * nt + t, ucol)),
            halo_spec,
            pl.BlockSpec(pool_w.shape, lambda b, t: (0, 0, 0)),
            pl.BlockSpec((1, POOL_WIDTH), lambda b, t: (0, 0)),
        ],
        out_specs=pl.BlockSpec((tb, POOL_WIDTH), lambda b, t: (b * nt + t, 0)),
        out_shape=jax.ShapeDtypeStruct((bsz * t_len, POOL_WIDTH), out_dtype),
        scratch_shapes=[pltpu.VMEM((HALO + tb, POOL_WIDTH), _F32)],
        compiler_params=_params(("parallel", "arbitrary"), blk, 4 * _nbytes((tb, POOL_WIDTH), _F32)),
        name=f"pool_t{tb}",
    )(z, halo, pool_w, pool_scale)


def _memattn_body(q_ref, k_hbm, v_hbm, o_ref, kv_buf, sems):
    b = pl.program_id(0)
    t = pl.program_id(1)
    hd = MEM_HEAD_DIM
    slot = b % 2

    def head_copies(bb, s):
        return [pltpu.make_async_copy(src.at[0, bb, :, h, :], kv_buf.at[s, i_src, h], sems.at[s])
                for i_src, src in enumerate((k_hbm, v_hbm)) for h in range(MEM_HEADS)]

    @pl.when(t == 0)
    def _():
        @pl.when(b == 0)
        def _():
            for c in head_copies(0, 0):
                c.start()

        for c in head_copies(b, slot):
            c.wait()

        @pl.when(b + 1 < pl.num_programs(0))
        def _():
            for c in head_copies(b + 1, 1 - slot):
                c.start()

    for h in range(MEM_HEADS):
        cols = slice(h * hd, (h + 1) * hd)
        q = (q_ref[:, cols] * (hd ** -0.5)).astype(_BF16)
        k = kv_buf[slot, 0, h].astype(_BF16)
        s = lax.dot_general(q, k, (((1,), (1,)), ((), ())), preferred_element_type=_F32)
        p = jnp.exp(s - jnp.max(s, axis=-1, keepdims=True))
        p = p / jnp.sum(p, axis=-1, keepdims=True)
        o = jnp.dot(p.astype(_BF16), kv_buf[slot, 1, h].astype(_BF16), preferred_element_type=_F32)
        o_ref[:, cols] = o.astype(o_ref.dtype)


def _memattn(z, mem_k, mem_v, *, row0, bsz, t_len, tb, out_dtype):
    nt = t_len // tb
    rb0 = row0 // tb
    m_tok = mem_k.shape[2]
    kv_shape = (2, 2, MEM_HEADS, m_tok, MEM_HEAD_DIM)
    blk = 2 * _nbytes((tb, MEM_WIDTH), _F32)
    return pl.pallas_call(
        _memattn_body,
        grid=(bsz, nt),
        in_specs=[pl.BlockSpec((tb, MEM_WIDTH), lambda b, t: (rb0 + b * nt + t, COL_MQ // MEM_WIDTH)),
                  pl.BlockSpec(memory_space=pl.ANY), pl.BlockSpec(memory_space=pl.ANY)],
        out_specs=pl.BlockSpec((tb, MEM_WIDTH), lambda b, t: (b * nt + t, 0)),
        out_shape=jax.ShapeDtypeStruct((bsz * t_len, MEM_WIDTH), out_dtype),
        scratch_shapes=[pltpu.VMEM(kv_shape, _F32), pltpu.SemaphoreType.DMA((2,))],
        compiler_params=_params(("arbitrary", "arbitrary"), blk,
                                _nbytes(kv_shape, _F32) + 6 * _nbytes((tb, m_tok), _F32)),
        name=f"memattn_t{tb}",
    )(z, mem_k, mem_v)


def _merge_body(a_ref, b_ref, c_ref, wa_ref, wb_ref, wc_ref, g0_ref, g1_ref, g2_ref,
                bg0_ref, bg1_ref, bg2_ref, o_ref):
    def gated(x_ref, w_ref, g_ref, bg_ref):
        br = jnp.dot(x_ref[...], w_ref[...], preferred_element_type=_F32)
        return jax.nn.sigmoid(g_ref[...] + bg_ref[...]) * br

    h = gated(a_ref, wa_ref, g0_ref, bg0_ref)
    h = h + gated(b_ref, wb_ref, g1_ref, bg1_ref)
    h = h + gated(c_ref, wc_ref, g2_ref, bg2_ref)
    o_ref[...] = h.astype(o_ref.dtype)


def _merge(br_a, br_b, br_c, wa, wb, wc, z, b_gate, bm, bn):
    n = br_a.shape[0]
    nj = D_MODEL // bn
    g0 = COL_G // bn

    def gspec(i_br):
        return pl.BlockSpec((bm, bn), lambda i, j: (i, g0 + i_br * nj + j))

    def bspec(i_br):
        return pl.BlockSpec((1, bn), lambda i, j: (0, i_br * nj + j))

    def xspec(width):
        return pl.BlockSpec((bm, width), lambda i, j: (i, 0))

    def wspec(width):
        return pl.BlockSpec((width, bn), lambda i, j: (0, j))

    blk = (_nbytes((bm, D_MODEL), _BF16) + _nbytes((D_MODEL, bn), _BF16) + 3 * _nbytes((bm, bn), _F32)
           + _nbytes((bm, bn), _BF16))
    return pl.pallas_call(
        _merge_body,
        grid=(n // bm, nj),
        in_specs=[xspec(GLA_DV), xspec(POOL_WIDTH), xspec(MEM_WIDTH),
                  wspec(GLA_DV), wspec(POOL_WIDTH), wspec(MEM_WIDTH),
                  gspec(0), gspec(1), gspec(2), bspec(0), bspec(1), bspec(2)],
        out_specs=pl.BlockSpec((bm, bn), lambda i, j: (i, j)),
        out_shape=jax.ShapeDtypeStruct((n, D_MODEL), _BF16),
        compiler_params=_params(("parallel", "arbitrary"), blk, 4 * _nbytes((bm, bn), _F32)),
        name="merge",
    )(br_a, br_b, br_c, wa, wb, wc, z, z, z, b_gate, b_gate, b_gate)


def _layer_norm(v, g, b):
    mu = jnp.mean(v, axis=-1, keepdims=True)
    d = v - mu
    var = jnp.mean(d * d, axis=-1, keepdims=True)
    return d * lax.rsqrt(var + NORM_EPS) * g + b


def _ln_router_body(x_ref, t_ref, g_ref, b_ref, wr_ref, br_ref, x1_ref, idx_ref, gate_ref):
    x1 = _layer_norm(DEEPNORM_ALPHA * x_ref[...] + t_ref[...], g_ref[...], b_ref[...])
    x1_ref[...] = x1
    logits = jnp.dot(x1, wr_ref[...], precision=_HI, preferred_element_type=_F32) + br_ref[...]
    bm = logits.shape[0]
    e_idx = lax.broadcasted_iota(jnp.int32, logits.shape, 1)
    lane = lax.broadcasted_iota(jnp.int32, (bm, LANE), 1)
    vals, idxs = [], []
    for _ in range(TOP_K):
        m = jnp.max(logits, axis=-1, keepdims=True)
        first = jnp.min(jnp.where(logits == m, e_idx, N_EXPERTS), axis=-1, keepdims=True)
        vals.append(m)
        idxs.append(first)
        logits = jnp.where(e_idx == first, -jnp.inf, logits)
    exps = [jnp.exp(v - vals[0]) for v in vals]
    denom = exps[0] + exps[1] + exps[2] + exps[3]
    idx_out = jnp.zeros((bm, LANE), jnp.int32)
    gate_out = jnp.zeros((bm, LANE), _F32)
    for k_sel in range(TOP_K):
        idx_out = jnp.where(lane == k_sel, idxs[k_sel], idx_out)
        gate_out = jnp.where(lane == k_sel, exps[k_sel] / denom, gate_out)
    idx_ref[...] = idx_out
    gate_ref[...] = gate_out


def _ln_router(x, t, ln_g, ln_b, w_router, b_router, bm):
    n = x.shape[0]
    row = pl.BlockSpec((bm, D_MODEL), lambda i: (i, 0))
    vec = pl.BlockSpec((1, D_MODEL), lambda i: (0, 0))
    small = pl.BlockSpec((bm, LANE), lambda i: (i, 0))
    blk = 3 * _nbytes((bm, D_MODEL), _F32) + _nbytes((D_MODEL, LANE), _F32)
    return pl.pallas_call(
        _ln_router_body,
        grid=(n // bm,),
        in_specs=[row, row, vec, vec,
                  pl.BlockSpec((D_MODEL, N_EXPERTS), lambda i: (0, 0)),
                  pl.BlockSpec((1, N_EXPERTS), lambda i: (0, 0))],
        out_specs=[row, small, small],
        out_shape=[jax.ShapeDtypeStruct((n, D_MODEL), _F32),
                   jax.ShapeDtypeStruct((n, LANE), jnp.int32), jax.ShapeDtypeStruct((n, LANE), _F32)],
        compiler_params=_params(("parallel",), blk, 8 * _nbytes((bm, D_MODEL), _F32)),
        name="ln1_router",
    )(x, t, ln_g, ln_b, w_router, b_router)


def _row_copy(src_hbm, row, dst, r, sem):
    return pltpu.make_async_copy(src_hbm.at[pl.ds(row, 1), :], dst.at[pl.ds(r, 1), :], sem)


def _dispatch_body(nlive_ref, tok_ref, tok_next_ref, x_hbm, o_ref, buf, sems, *, gb):
    i = pl.program_id(0)
    n_live = nlive_ref[0]
    slot = i % 2

    def issue(tok, s):
        def one(r, carry):
            _row_copy(x_hbm, tok[0, 0, r], buf.at[s], r, sems.at[s]).start()
            return carry
        lax.fori_loop(0, gb, one, 0, unroll=ROW_DMA_UNROLL)

    @pl.when((i == 0) & (n_live > 0))
    def _():
        issue(tok_ref, 0)

    @pl.when(i + 1 < jnp.minimum(n_live, pl.num_programs(0)))
    def _():
        issue(tok_next_ref, 1 - slot)

    @pl.when(i < n_live)
    def _():
        def drain(r, carry):
            _row_copy(x_hbm, 0, buf.at[slot], r, sems.at[slot]).wait()
            return carry

        lax.fori_loop(0, gb, drain, 0, unroll=ROW_DMA_UNROLL)
        o_ref[...] = buf[slot].astype(o_ref.dtype)

    @pl.when(i >= n_live)
    def _():
        o_ref[...] = jnp.zeros(o_ref.shape, o_ref.dtype)


def _dispatch(x1, slot_tok, n_live_blocks, gb):
    p = slot_tok.shape[0]
    nb = p // gb
    tok3 = slot_tok.reshape(nb, 1, gb)
    blk = _nbytes((gb, D_MODEL), _BF16)
    return pl.pallas_call(
        functools.partial(_dispatch_body, gb=gb),
        grid_spec=pltpu.PrefetchScalarGridSpec(
            num_scalar_prefetch=1,
            grid=(nb,),
            in_specs=[pl.BlockSpec((1, 1, gb), lambda i, nl: (i, 0, 0), memory_space=pltpu.SMEM),
                      pl.BlockSpec((1, 1, gb), lambda i, nl: (jnp.minimum(i + 1, nb - 1), 0, 0),
                                   memory_space=pltpu.SMEM),
                      pl.BlockSpec(memory_space=pl.ANY)],
            out_specs=pl.BlockSpec((gb, D_MODEL), lambda i, nl: (i, 0)),
            scratch_shapes=[pltpu.VMEM((2, gb, D_MODEL), _F32), pltpu.SemaphoreType.DMA((2,))],
        ),
        out_shape=jax.ShapeDtypeStruct((p, D_MODEL), _BF16),
        compiler_params=_params(("arbitrary",), blk, 3 * _nbytes((gb, D_MODEL), _F32)),
        name="moe_dispatch",
    )(n_live_blocks, tok3, tok3, x1)


def _pair_order(p):
    return jnp.where((p & 1) == 0, p >> 1, (LANE // 2) + (p >> 1))


def _refresh_weight_tile(step, we_ref, wjw_ref, fresh_ref, next_e_ref, next_j_ref, has_next_ref,
                         w_hbm, stage, w_scr, sem):
    cols = stage.shape[1]

    def tile_copy(e, j):
        return pltpu.make_async_copy(w_hbm.at[0, e, :, pl.ds(pl.multiple_of(j * cols, cols), cols)], stage, sem)

    @pl.when(fresh_ref[step] == 1)
    def _():
        @pl.when(step == 0)
        def _():
            tile_copy(we_ref[0], wjw_ref[0]).start()

        tile_copy(we_ref[step], wjw_ref[step]).wait()

        def convert(rc, carry):
            rows = pl.ds(pl.multiple_of(rc * CONV_ROWS, CONV_ROWS), CONV_ROWS)
            w_scr[rows, :] = stage[rows, :].astype(_BF16)
            return carry

        lax.fori_loop(0, stage.shape[0] // CONV_ROWS, convert, 0)

        @pl.when(has_next_ref[step] == 1)
        def _():
            tile_copy(next_e_ref[step], next_j_ref[step]).start()


def _up_body(we_ref, wjw_ref, wr_ref, wjo_ref, live_ref, fresh_ref, next_e_ref, next_j_ref, has_next_ref,
             x_ref, w_hbm, b_ref, o_ref, stage, w_scr, sem, *, tn):
    step = pl.program_id(0)
    live = live_ref[step] == 1
    wide = 2 * LANE
    _refresh_weight_tile(step, we_ref, wjw_ref, fresh_ref, next_e_ref, next_j_ref, has_next_ref,
                         w_hbm, stage, w_scr, sem)

    @pl.when(jnp.logical_not(live))
    def _():
        o_ref[...] = jnp.zeros(o_ref.shape, o_ref.dtype)

    @pl.when(live)
    def _():
        h = jnp.dot(x_ref[...], w_scr[...], preferred_element_type=_F32) + b_ref[...]
        even = (lax.broadcasted_iota(jnp.int32, (h.shape[0], LANE), 1) & 1) == 0
        p_in = lax.broadcasted_iota(jnp.int32, (wide, wide), 0)
        p_out = lax.broadcasted_iota(jnp.int32, (wide, wide), 1)
        same_grp = (p_in >> LANE_BITS) == (p_out >> LANE_BITS)
        unshuffle = jnp.where(same_grp & (_pair_order(p_in & (LANE - 1)) == (p_out & (LANE - 1))), 1.0, 0.0)
        unshuffle = unshuffle.astype(_BF16)
        for m2 in range(tn // wide):
            acts = []
            for m in (2 * m2, 2 * m2 + 1):
                a = h[:, (2 * m) * LANE:(2 * m + 1) * LANE]
                b = h[:, (2 * m + 1) * LANE:(2 * m + 2) * LANE]
                h_glu = jnp.where(even, a, pltpu.roll(b, 1, axis=1))
                h_lin = jnp.where(even, pltpu.roll(a, LANE - 1, axis=1), b)
                h_glu = jnp.minimum(h_glu, SWIGLU_LIMIT)
                h_lin = jnp.clip(h_lin, -SWIGLU_LIMIT, SWIGLU_LIMIT)
                acts.append((h_glu * jax.nn.sigmoid(SWIGLU_ALPHA * h_glu) * (h_lin + 1.0)).astype(_BF16))
            act = jnp.concatenate(acts, axis=1)
            o_ref[:, m2 * wide:(m2 + 1) * wide] = jnp.dot(act, unshuffle,
                                                          preferred_element_type=_F32).astype(o_ref.dtype)


def _down_body(we_ref, wjw_ref, wr_ref, wjo_ref, live_ref, fresh_ref, next_e_ref, next_j_ref, has_next_ref,
               x_ref, w_hbm, bd_ref, o_ref, stage, wd_scr, sem):
    step = pl.program_id(0)
    live = live_ref[step] == 1
    _refresh_weight_tile(step, we_ref, wjw_ref, fresh_ref, next_e_ref, next_j_ref, has_next_ref,
                         w_hbm, stage, wd_scr, sem)

    @pl.when(jnp.logical_not(live))
    def _():
        o_ref[...] = jnp.zeros(o_ref.shape, o_ref.dtype)

    @pl.when(live)
    def _():
        o_ref[...] = jnp.dot(x_ref[...], wd_scr[...], preferred_element_type=_F32) + bd_ref[...]


def _work_b(w, we, wjw, wr, wjo, *_):
    return (we[w], 0, wjw[w])


def _work_x(w, we, wjw, wr, wjo, *_):
    return (wr[w], 0)


def _work_o(w, we, wjw, wr, wjo, *_):
    return (wr[w], wjo[w])


def _moe_up(work, xs, w_up, b_up, bm, tn):
    p = xs.shape[0]
    n_work = work[0].shape[0]
    blk = _nbytes((bm, D_MODEL), _BF16) + _nbytes((bm, tn), _BF16)
    scr = _nbytes((D_MODEL, 2 * tn), _F32) + _nbytes((D_MODEL, 2 * tn), _BF16)
    return pl.pallas_call(
        functools.partial(_up_body, tn=tn),
        grid_spec=pltpu.PrefetchScalarGridSpec(
            num_scalar_prefetch=len(work),
            grid=(n_work,),
            in_specs=[pl.BlockSpec((bm, D_MODEL), _work_x),
                      pl.BlockSpec(memory_space=pl.ANY),
                      pl.BlockSpec((None, 1, 2 * tn), _work_b)],
            out_specs=pl.BlockSpec((bm, tn), _work_o),
            scratch_shapes=[pltpu.VMEM((D_MODEL, 2 * tn), _F32), pltpu.VMEM((D_MODEL, 2 * tn), _BF16),
                            pltpu.SemaphoreType.DMA(())],
        ),
        out_shape=jax.ShapeDtypeStruct((p, D_FF), _BF16),
        compiler_params=_params(("arbitrary",), blk, scr + 6 * _nbytes((bm, 2 * tn), _F32)),
        name="moe_up",
    )(*work, xs, w_up, b_up)


def _moe_down(work, hact, w_down, bd, bm, tn):
    p = hact.shape[0]
    n_work = work[0].shape[0]
    blk = _nbytes((bm, D_FF), _BF16) + _nbytes((bm, tn), _F32)
    scr = _nbytes((D_FF, tn), _F32) + _nbytes((D_FF, tn), _BF16)
    return pl.pallas_call(
        _down_body,
        grid_spec=pltpu.PrefetchScalarGridSpec(
            num_scalar_prefetch=len(work),
            grid=(n_work,),
            in_specs=[pl.BlockSpec((bm, D_FF), _work_x),
                      pl.BlockSpec(memory_space=pl.ANY),
                      pl.BlockSpec((None, 1, tn), _work_b)],
            out_specs=pl.BlockSpec((bm, tn), _work_o),
            scratch_shapes=[pltpu.VMEM((D_FF, tn), _F32), pltpu.VMEM((D_FF, tn), _BF16),
                            pltpu.SemaphoreType.DMA(())],
        ),
        out_shape=jax.ShapeDtypeStruct((p, D_MODEL), _F32),
        compiler_params=_params(("arbitrary",), blk, scr + 4 * _nbytes((bm, tn), _F32)),
        name="moe_down",
    )(*work, hact, w_down, bd)


def _combine_body(pos_ref, yd_hbm, gate_ref, x1_ref, g_ref, b_ref, o_ref, buf, sem, *, tb):
    def issue(t, carry):
        for k_sel in range(TOP_K):
            _row_copy(yd_hbm, pos_ref[0, 0, t * TOP_K + k_sel], buf.at[k_sel], t, sem).start()
        return carry

    def drain(t, carry):
        for k_sel in range(TOP_K):
            _row_copy(yd_hbm, 0, buf.at[k_sel], t, sem).wait()
        return carry

    lax.fori_loop(0, tb, issue, 0)
    lax.fori_loop(0, tb, drain, 0)
    gate = gate_ref[...]
    ffn = buf[0] * gate[:, 0:1]
    for k_sel in range(1, TOP_K):
        ffn = ffn + buf[k_sel] * gate[:, k_sel:k_sel + 1]
    o_ref[...] = _layer_norm(DEEPNORM_ALPHA * x1_ref[...] + ffn, g_ref[...], b_ref[...])


def _combine(yd, pos, gate_pad, x1, ln_g, ln_b, *, row0, n_rows, tb):
    nb = n_rows // tb
    rb0 = row0 // tb
    pos3 = pos.reshape(-1, 1, tb * TOP_K)
    vec = pl.BlockSpec((1, D_MODEL), lambda i: (0, 0))
    blk = 2 * _nbytes((tb, D_MODEL), _F32) + _nbytes((tb, LANE), _F32)
    return pl.pallas_call(
        functools.partial(_combine_body, tb=tb),
        grid=(nb,),
        in_specs=[pl.BlockSpec((1, 1, tb * TOP_K), lambda i: (rb0 + i, 0, 0), memory_space=pltpu.SMEM),
                  pl.BlockSpec(memory_space=pl.ANY),
                  pl.BlockSpec((tb, LANE), lambda i: (rb0 + i, 0)),
                  pl.BlockSpec((tb, D_MODEL), lambda i: (rb0 + i, 0)), vec, vec],
        out_specs=pl.BlockSpec((tb, D_MODEL), lambda i: (i, 0)),
        out_shape=jax.ShapeDtypeStruct((n_rows, D_MODEL), _F32),
        scratch_shapes=[pltpu.VMEM((TOP_K, tb, D_MODEL), _F32), pltpu.SemaphoreType.DMA(())],
        compiler_params=_params(("arbitrary",), blk, (TOP_K + 4) * _nbytes((tb, D_MODEL), _F32)),
        name=f"moe_combine_ln2_r{row0}",
    )(pos3, yd, gate_pad, x1, ln_g, ln_b)


def _routing(top_idx, bm):
    n_tok = top_idx.shape[0]
    n_assign = n_tok * TOP_K
    assert n_assign % LANE == 0
    n_chunks = n_assign // LANE
    n_blocks_max = (n_assign + N_EXPERTS * (bm - 1) + bm - 1) // bm
    flat_e = top_idx.reshape(-1)
    flat_tok = jnp.repeat(jnp.arange(n_tok, dtype=jnp.int32), TOP_K)
    onehot = flat_e[:, None] == jnp.arange(N_EXPERTS, dtype=jnp.int32)[None, :]
    oh3 = onehot.reshape(n_chunks, LANE, N_EXPERTS).astype(_BF16)
    incl = jnp.tril(jnp.ones((LANE, LANE), _BF16))
    within = jnp.einsum('ij,cje->cie', incl, oh3, preferred_element_type=_F32)
    chunk_tot = within[:, -1, :]
    before = jnp.tril(jnp.ones((n_chunks, n_chunks), _BF16), k=-1)
    chunk_off = jnp.dot(before, chunk_tot.astype(_BF16), preferred_element_type=_F32)
    csum = (within + chunk_off[:, None, :]).reshape(n_assign, N_EXPERTS)
    rank = jnp.sum(jnp.where(onehot, csum, 0.0), axis=1).astype(jnp.int32) - 1
    counts = (chunk_off[-1] + chunk_tot[-1]).astype(jnp.int32)
    nblk_e = (counts + bm - 1) // bm
    blk_end = jnp.cumsum(nblk_e)
    blk_start = blk_end - nblk_e
    dest = (blk_start[flat_e] * bm + rank).astype(jnp.int32)
    slot_tok = jnp.zeros((n_blocks_max * bm,), jnp.int32).at[dest].set(flat_tok, unique_indices=True)
    n_live_blocks = blk_end[-1].astype(jnp.int32)
    return slot_tok, dest.reshape(n_tok, TOP_K), n_live_blocks, (nblk_e, blk_start, blk_end, n_blocks_max)


def _work_list(layout, n_live_blocks, n_col_tiles):
    nblk_e, blk_start, blk_end, n_blocks_max = layout
    n_work = n_blocks_max * n_col_tiles
    n_live_work = n_live_blocks * n_col_tiles
    w_all = jnp.arange(n_work, dtype=jnp.int32)
    live = w_all < n_live_work
    w = jnp.minimum(w_all, n_live_work - 1)
    work_end = blk_end * n_col_tiles
    we = jnp.minimum(jnp.sum((work_end[None, :] <= w[:, None]).astype(jnp.int32), axis=1), N_EXPERTS - 1)
    local = w - blk_start[we] * n_col_tiles
    nb_e = jnp.maximum(nblk_e[we], 1)
    wjw = (local // nb_e).astype(jnp.int32)
    dead = w_all - n_live_work
    wr = jnp.where(live, blk_start[we] + local % nb_e, n_live_blocks + dead // n_col_tiles).astype(jnp.int32)
    wjo = jnp.where(live, wjw, dead % n_col_tiles).astype(jnp.int32)
    tile_id = we * n_col_tiles + wjw
    fresh = live & jnp.concatenate([jnp.ones((1,), bool), tile_id[1:] != tile_id[:-1]])
    next_w = w_all + nblk_e[we]
    has_next = fresh & (next_w < n_live_work)
    next_w = jnp.minimum(next_w, n_work - 1)
    as_i32 = lambda a: a.astype(jnp.int32)
    return (as_i32(we), wjw, wr, wjo, as_i32(live), as_i32(fresh), as_i32(we[next_w]), wjw[next_w], as_i32(has_next))


def kernel(x_prompt, x_sample, mem_prompt, cache_mem_k, cache_mem_v, state_gla, state_pool, w_in, w_fg, b_fg, gla_norm_g, pool_w, pool_scale, w_mem_k, w_mem_v, w_br_gla, w_br_pool, w_br_mem, b_gate, w_out, ln1_g, ln1_b, w_router, b_router, w_up, b_up, w_down, b_down, ln2_g, ln2_b):
    assert w_in.shape[0] == DEPTH == 1
    bp, tp, _ = x_prompt.shape
    bs, ts, _ = x_sample.shape
    n_p, n_s = bp * tp, bs * ts
    n_tok = n_p + n_s
    m_tok = mem_prompt.shape[1]

    x_all = jnp.concatenate([x_prompt.reshape(n_p, D_MODEL), x_sample.reshape(n_s, D_MODEL)], axis=0)
    x_bf = x_all.astype(_BF16)
    w_fgl = jnp.pad(w_in[0, :, FG_COL0:FG_COL0 + GLA_GATE_RANK], ((0, 0), (0, FG_PAD - GLA_GATE_RANK))).astype(_BF16)
    wfg_pad = jnp.pad(w_fg[0], ((0, FG_PAD - GLA_GATE_RANK), (0, 0)))
    b_fg2 = b_fg[0].reshape(1, GLA_DK)
    gla_g = gla_norm_g[0].reshape(1, GLA_DV)

    z = _proj(x_bf, w_in, col0=0, width=ZA_WIDTH, bm=1024, bn=512, name="in_proj_a")
    zb = _proj(x_bf, w_in, col0=ZB_COL0, width=ZB_WIDTH, bm=1024, bn=512, name="in_proj_b")
    fgl = _matmul(x_bf, w_fgl, 1024, FG_PAD, name="in_proj_fg")

    mem_bf = mem_prompt.reshape(bp * m_tok, D_MODEL).astype(_BF16)
    mk = _matmul(mem_bf, w_mem_k[0].astype(_BF16), 512, 512, name="mem_k")
    mv = _matmul(mem_bf, w_mem_v[0].astype(_BF16), 512, 512, name="mem_v")
    mk_out = mk.reshape(1, bp, m_tok, MEM_HEADS, MEM_HEAD_DIM)
    mv_out = mv.reshape(1, bp, m_tok, MEM_HEADS, MEM_HEAD_DIM)

    s0_p = jnp.zeros((1, bp, GLA_HEADS, GLA_HEAD_DK, GLA_HEAD_DV), _F32)
    oa_p, gla_p = _gla(z, fgl, wfg_pad, b_fg2, gla_g, s0_p, row0=0, bsz=bp, t_len=tp, tb=256, c=GLA_SUB,
                       out_dtype=_BF16)
    oa_s, gla_s = _gla(z, fgl, wfg_pad, b_fg2, gla_g, state_gla, row0=n_p, bsz=bs, t_len=ts, tb=ts, c=ts,
                       out_dtype=_F32)

    scale2 = pool_scale[0].reshape(1, POOL_WIDTH)
    ob_p = _pool(zb, zb, pool_w[0], scale2, row0=0, bsz=bp, t_len=tp, tb=512, pos0=0, halo_from_z=True,
                 out_dtype=_BF16)
    halo_s = jnp.pad(state_pool[0], ((0, 0), (HALO - POOL_BUF, 0), (0, 0)))
    ob_s = _pool(zb, halo_s, pool_w[0], scale2, row0=n_p, bsz=bs, t_len=ts, tb=ts, pos0=PAST_LEN, halo_from_z=False,
                 out_dtype=_F32)

    oc_p = _memattn(zb, mk_out, mv_out, row0=0, bsz=bp, t_len=tp, tb=512, out_dtype=_BF16)
    oc_s = _memattn(zb, cache_mem_k, cache_mem_v, row0=n_p, bsz=bs, t_len=ts, tb=ts, out_dtype=_F32)

    br_a = jnp.concatenate([oa_p, oa_s.astype(_BF16)], axis=0)
    br_b = jnp.concatenate([ob_p, ob_s.astype(_BF16)], axis=0)
    br_c = jnp.concatenate([oc_p, oc_s.astype(_BF16)], axis=0)

    h = _merge(br_a, br_b, br_c, w_br_gla[0].astype(_BF16), w_br_pool[0].astype(_BF16), w_br_mem[0].astype(_BF16),
               zb, b_gate[0].reshape(1, N_BRANCHES * D_MODEL), 1024, 512)
    t_out = _matmul(h, w_out[0].astype(_BF16), 1024, 1024, name="out_proj")
    x1, idx_pad, gate_pad = _ln_router(x_all, t_out, ln1_g[0].reshape(1, D_MODEL), ln1_b[0].reshape(1, D_MODEL),
                                       w_router[0], b_router[0].reshape(1, N_EXPERTS), 256)
    top_idx = idx_pad[:, :TOP_K]

    slot_tok, pos, n_live_blocks, layout = _routing(top_idx, MOE_BM)
    xs = _dispatch(x1, slot_tok, n_live_blocks.reshape(1), MOE_BM)
    hact = _moe_up(_work_list(layout, n_live_blocks, D_FF // MOE_UP_TN), xs, w_up,
                   b_up[0].reshape(N_EXPERTS, 1, 2 * D_FF), MOE_BM, MOE_UP_TN)
    yd = _moe_down(_work_list(layout, n_live_blocks, D_MODEL // MOE_DOWN_TN), hact, w_down,
                   b_down[0].reshape(N_EXPERTS, 1, D_MODEL), MOE_BM, MOE_DOWN_TN)
    ln2 = (ln2_g[0].reshape(1, D_MODEL), ln2_b[0].reshape(1, D_MODEL))
    y_p = _combine(yd, pos, gate_pad, x1, *ln2, row0=0, n_rows=n_p, tb=128).reshape(bp, tp, D_MODEL)
    y_s = _combine(yd, pos, gate_pad, x1, *ln2, row0=n_p, n_rows=n_s, tb=128).reshape(bs, ts, D_MODEL)

    u_p = zb[:n_p, COL_U:COL_U + POOL_WIDTH].reshape(bp, tp, POOL_WIDTH)
    u_s = zb[n_p:, COL_U:COL_U + POOL_WIDTH].reshape(bs, ts, POOL_WIDTH)
    pool_p = u_p[:, tp - POOL_BUF:]
    pool_s = jnp.concatenate([state_pool[0], u_s], axis=1)[:, -POOL_BUF:]
    return (y_p, y_s, mk_out, mv_out, gla_p, pool_p[None], gla_s, pool_s[None])
```
